```python
import math
import jax, jax.numpy as jnp
from jax import lax
import numpy as np

D_MODEL = 2048
BATCH = 2
SEQ = 4096
DEPTH = 2

N_EVEN = (DEPTH + 1) // 2
N_ODD = DEPTH // 2
PLE_DIM = 256
D_FF = 5632
NORM_EPS = 1e-6
POOL_WINDOWS = (2, 4, 8, 16)
POOL_GROUPS = 4
POOL_WIDTH = D_MODEL // 2
POOL_GROUP_DIM = POOL_WIDTH // POOL_GROUPS
DIFF_HEAD_DIM = 128
DIFF_HEADS = D_MODEL // (4 * DIFF_HEAD_DIM)
DIFF_WIDTH = DIFF_HEADS * 2 * DIFF_HEAD_DIM
ROPE_THETA = 500000.0
ROT_DIM = DIFF_HEAD_DIM // 4
Q_BLOCK = 128
SUBLN_EPS = 1e-5
AB_IN_WIDTH = POOL_WIDTH + 3 * DIFF_WIDTH
AB_OUT_WIDTH = POOL_WIDTH + DIFF_WIDTH
HGRN_EXPAND = 128
HGRN_HEADS = D_MODEL // HGRN_EXPAND
HGRN_DK = HGRN_EXPAND
HGRN_DV = D_MODEL // HGRN_HEADS
HGRN_CHUNK = 64

kernel_name = "hybrid_pool_diffattn_hgrn2_macaron"


def rmsnorm(x, gain, eps=NORM_EPS):
    xf = x.astype(jnp.float32)
    y = xf * lax.rsqrt(jnp.mean(xf * xf, axis=-1, keepdims=True) + eps) * gain.astype(jnp.float32)
    return y.astype(x.dtype)


def swiglu(h, w_gate, w_up, w_down):
    return (jax.nn.silu(h @ w_gate) * (h @ w_up)) @ w_down


def rope_tables(positions):
    inv_freq = ROPE_THETA ** (-jnp.arange(0, ROT_DIM, 2, dtype=jnp.float32) / ROT_DIM)
    ang = positions.astype(jnp.float32)[..., None] * inv_freq
    return jnp.cos(ang), jnp.sin(ang)


def partial_rope(t, cos, sin):
    c = cos[:, :, None, None, :]
    s = sin[:, :, None, None, :]
    half = ROT_DIM // 2
    x1 = t[..., :half]
    x2 = t[..., half:ROT_DIM]
    rot = jnp.concatenate([x1 * c - x2 * s, x2 * c + x1 * s], axis=-1).astype(t.dtype)
    return jnp.concatenate([rot, t[..., ROT_DIM:]], axis=-1)


def multiscale_pool(u, pool_w, pool_scale):
    B, S, _ = u.shape
    ug = u.reshape(B, S, POOL_GROUPS, POOL_GROUP_DIM)
    t = jnp.arange(1, S + 1, dtype=jnp.float32)
    outs = []
    for g, w in enumerate(POOL_WINDOWS):
        v = ug[:, :, g].astype(jnp.float32)
        cs = jnp.cumsum(v, axis=1)
        lag = jnp.pad(cs, ((0, 0), (w, 0), (0, 0)))[:, :S]
        mean = (cs - lag) / jnp.minimum(t, float(w))[None, :, None]
        outs.append(mean - v)
    d = jnp.stack(outs, axis=2).astype(u.dtype)
    y = jnp.einsum('bsgc,gcd->bsgd', d, pool_w).reshape(B, S, POOL_WIDTH)
    return y * pool_scale


def diff_attention(q, k, v, cos, sin, lam_params, subln, lam_init):
    B, S, _ = q.shape
    H, DH = DIFF_HEADS, DIFF_HEAD_DIM
    q = partial_rope(q.reshape(B, S, H, 2, DH), cos, sin)
    k = partial_rope(k.reshape(B, S, H, 2, DH), cos, sin)
    v = v.reshape(B, S, H, 2 * DH)
    lp = lam_params.astype(jnp.float32)
    lam = jnp.exp(jnp.sum(lp[0] * lp[1])) - jnp.exp(jnp.sum(lp[2] * lp[3])) + lam_init
    scale = DH ** -0.5
    kt = k.transpose(0, 2, 3, 1, 4)
    vt = v.transpose(0, 2, 1, 3)
    nb = S // Q_BLOCK
    qb = q.reshape(B, nb, Q_BLOCK, H, 2, DH).transpose(1, 0, 3, 4, 2, 5)
    kpos = jnp.arange(S)

    def one_block(args):
        q_blk, b_idx = args
        qpos = b_idx * Q_BLOCK + jnp.arange(Q_BLOCK)
        s = jnp.einsum('bhcqd,bhckd->bhcqk', q_blk, kt).astype(jnp.float32) * scale
        s = jnp.where(kpos[None, :] <= qpos[:, None], s, -jnp.inf)
        pr = jax.nn.softmax(s, axis=-1)
        a = pr[:, :, 0] - lam * pr[:, :, 1]
        return jnp.einsum('bhqk,bhke->bhqe', a.astype(vt.dtype), vt)

    o = lax.map(one_block, (qb, jnp.arange(nb)))
    o = o.transpose(1, 0, 3, 2, 4).reshape(B, S, H, 2 * DH)
    o = rmsnorm(o, subln, SUBLN_EPS) * (1.0 - lam_init)
    return o.reshape(B, S, DIFF_WIDTH)


def hgrn2_scan(q, k, v, log_f):
    B, S, H, DK = q.shape
    DV = v.shape[-1]
    C = HGRN_CHUNK
    nc = S // C

    def to_chunks(t):
        return t.reshape(B, nc, C, H, t.shape[-1]).transpose(1, 0, 3, 2, 4)

    qc, kc, vc, gc = to_chunks(q), to_chunks(k), to_chunks(v), to_chunks(log_f)
    causal = jnp.tril(jnp.ones((C, C), dtype=bool))

    def step(state, inp):
        q_c, k_c, v_c, g_c = inp
        G = jnp.cumsum(g_c, axis=2)
        o_inter = jnp.einsum('bhtd,bhde->bhte', q_c * jnp.exp(G), state)
        diff = G[:, :, :, None, :] - G[:, :, None, :, :]
        decay = jnp.exp(jnp.where(causal[None, None, :, :, None], diff, -jnp.inf))
        A = jnp.einsum('bhtd,bhtsd,bhsd->bhts', q_c, decay, k_c)
        o_intra = jnp.einsum('bhts,bhse->bhte', A, v_c)
        G_last = G[:, :, -1, :]
        new_state = jnp.exp(G_last)[..., None] * state + jnp.einsum(
            'bhsd,bhse->bhde', k_c * jnp.exp(G_last[:, :, None, :] - G), v_c)
        return new_state, (o_inter + o_intra).astype(v.dtype)

    s0 = jnp.zeros((B, H, DK, DV), jnp.float32)
    _, oc = lax.scan(step, s0, (qc, kc, vc, gc))
    return oc.transpose(1, 0, 3, 2, 4).reshape(B, S, H, DV)


def hgrn2_mixer(h, w_in, lower_bound, norm_gain, w_out):
    B, S, _ = h.shape
    q, f, i, g = jnp.split(h @ w_in, 4, axis=-1)
    q = jax.nn.silu(q)
    ff = f.astype(jnp.float32)
    log_f = jnp.logaddexp(jnp.log(lower_bound), jnp.log1p(-lower_bound) + jax.nn.log_sigmoid(ff))
    k = -jnp.expm1(log_f)
    H = HGRN_HEADS
    o = hgrn2_scan(q.reshape(B, S, H, HGRN_DK), k.reshape(B, S, H, HGRN_DK),
                   i.reshape(B, S, H, HGRN_DV), log_f.reshape(B, S, H, HGRN_DK))
    o = rmsnorm(o, norm_gain) * jax.nn.silu(g).reshape(B, S, H, HGRN_DV)
    return o.reshape(B, S, D_MODEL) @ w_out


def setup_inputs(seed: int = 0) -> dict:
    key = jax.random.key(seed)
    ks = iter(jax.random.split(key, 40))
    f32 = jnp.float32

    def w(shape, fan_in):
        return jax.random.normal(next(ks), shape, f32) * fan_in ** -0.5

    def gain(shape):
        return 1.0 + 0.02 * jax.random.normal(next(ks), shape, f32)

    x = jax.random.normal(next(ks), (BATCH, SEQ, D_MODEL), f32)
    p = jax.random.normal(next(ks), (DEPTH, BATCH, SEQ, PLE_DIM), f32)
    offs = jax.random.randint(next(ks), (BATCH, 1), 0, 1024, dtype=jnp.int32)
    positions = (offs + jnp.arange(SEQ, dtype=jnp.int32)[None, :]).astype(jnp.int32)
    return {
        "x": x,
        "p": p,
        "positions": positions,
        "ffn1_norm": gain((DEPTH, D_MODEL)),
        "ffn1_w_gate": w((DEPTH, D_MODEL, D_FF), D_MODEL),
        "ffn1_w_up": w((DEPTH, D_MODEL, D_FF), D_MODEL),
        "ffn1_w_down": w((DEPTH, D_FF, D_MODEL), D_FF),
        "mix_norm": gain((DEPTH, D_MODEL)),
        "ffn2_norm": gain((DEPTH, D_MODEL)),
        "ffn2_w_gate": w((DEPTH, D_MODEL, D_FF), D_MODEL),
        "ffn2_w_up": w((DEPTH, D_MODEL, D_FF), D_MODEL),
        "ffn2_w_down": w((DEPTH, D_FF, D_MODEL), D_FF),
        "ple_norm": gain((DEPTH, D_MODEL)),
        "ple_w_gate": w((DEPTH, D_MODEL, D_MODEL), D_MODEL),
        "ple_w_proj": w((DEPTH, PLE_DIM, D_MODEL), PLE_DIM),
        "ab_w_in": w((N_EVEN, D_MODEL, AB_IN_WIDTH), D_MODEL),
        "pool_w": w((N_EVEN, POOL_GROUPS, POOL_GROUP_DIM, POOL_GROUP_DIM), POOL_GROUP_DIM),
        "pool_scale": gain((N_EVEN, POOL_WIDTH)),
        "diff_lambda": 0.1 * jax.random.normal(next(ks), (N_EVEN, 4, DIFF_HEAD_DIM), f32),
        "diff_subln": gain((N_EVEN, 2 * DIFF_HEAD_DIM)),
        "ab_w_out": w((N_EVEN, AB_OUT_WIDTH, D_MODEL), AB_OUT_WIDTH),
        "hgrn_w_in": w((N_ODD, D_MODEL, 4 * D_MODEL), D_MODEL),
        "hgrn_lower_bounds": 1.0 + 0.1 * jax.random.normal(next(ks), (DEPTH, D_MODEL), f32),
        "hgrn_norm": gain((N_ODD, HGRN_DV)),
        "hgrn_w_out": w((N_ODD, D_MODEL, D_MODEL), D_MODEL),
        "final_norm": gain((D_MODEL,)),
    }


def reference(x, p, positions, ffn1_norm, ffn1_w_gate, ffn1_w_up, ffn1_w_down, mix_norm,
              ffn2_norm, ffn2_w_gate, ffn2_w_up, ffn2_w_down, ple_norm, ple_w_gate, ple_w_proj,
              ab_w_in, pool_w, pool_scale, diff_lambda, diff_subln, ab_w_out,
              hgrn_w_in, hgrn_lower_bounds, hgrn_norm, hgrn_w_out, final_norm):
    cos, sin = rope_tables(positions)
    lbs = jax.nn.softmax(hgrn_lower_bounds.astype(jnp.float32), axis=0)
    lbs = jnp.cumsum(lbs, axis=0) - lbs[0]
    for i in range(DEPTH):
        x = x + 0.5 * swiglu(rmsnorm(x, ffn1_norm[i]), ffn1_w_gate[i], ffn1_w_up[i], ffn1_w_down[i])
        h = rmsnorm(x, mix_norm[i])
        if i % 2 == 0:
            e = i // 2
            proj = h @ ab_w_in[e]
            u = proj[..., :POOL_WIDTH]
            q = proj[..., POOL_WIDTH:POOL_WIDTH + DIFF_WIDTH]
            k = proj[..., POOL_WIDTH + DIFF_WIDTH:POOL_WIDTH + 2 * DIFF_WIDTH]
            v = proj[..., POOL_WIDTH + 2 * DIFF_WIDTH:]
            lam_init = 0.8 - 0.6 * math.exp(-0.3 * i)
            a_out = multiscale_pool(u, pool_w[e], pool_scale[e])
            b_out = diff_attention(q, k, v, cos, sin, diff_lambda[e], diff_subln[e], lam_init)
            mix = jnp.concatenate([a_out, b_out], axis=-1) @ ab_w_out[e]
        else:
            o = i // 2
            mix = hgrn2_mixer(h, hgrn_w_in[o], lbs[i], hgrn_norm[o], hgrn_w_out[o])
        x = x + mix.astype(x.dtype)
        x = x + 0.5 * swiglu(rmsnorm(x, ffn2_norm[i]), ffn2_w_gate[i], ffn2_w_up[i], ffn2_w_down[i])
        gate = jax.nn.sigmoid(rmsnorm(x, ple_norm[i]) @ ple_w_gate[i])
        x = x + (p[i] @ ple_w_proj[i]) * gate
    return rmsnorm(x, final_norm)
```

```python
import functools
import math

import jax
import jax.numpy as jnp
from jax import lax
from jax.experimental import pallas as pl
from jax.experimental.pallas import tpu as pltpu

F32 = jnp.float32
BF16 = jnp.bfloat16

NORM_EPS = 1e-6
SUBLN_EPS = 1e-5
POOL_WINDOWS = (2, 4, 8, 16)
POOL_HALO = 16
DIFF_HEAD_DIM = 128
ROT_DIM = DIFF_HEAD_DIM // 4
ROPE_THETA = 500000.0
HGRN_HEAD_DIM = 128
HGRN_CHUNK = 16

LANES = 128
VMEM_LIMIT = 56 * 1024 * 1024


def _cparams(sem):
    return pltpu.CompilerParams(dimension_semantics=sem, vmem_limit_bytes=VMEM_LIMIT)


def _rms(x, gain, eps):
    ms = jnp.mean(x * x, axis=-1, keepdims=True)
    return x * lax.rsqrt(ms + eps) * gain


def _dot(a, b):
    return jnp.dot(a, b, preferred_element_type=F32)


def _dot_nt(a, b):
    return lax.dot_general(a, b, (((1,), (1,)), ((), ())), preferred_element_type=F32)


def _dot_tn(a, b):
    return lax.dot_general(a, b, (((0,), (0,)), ((), ())), preferred_element_type=F32)


def _ffn_kernel(x_ref, g_ref, wg_ref, wu_ref, wd_ref, o_ref, h_ref):
    j = pl.program_id(1)

    @pl.when(j == 0)
    def _():
        h_ref[...] = _rms(x_ref[...], g_ref[...], NORM_EPS).astype(BF16)
        o_ref[...] = jnp.zeros_like(o_ref)

    h = h_ref[...]
    a = _dot(h, wg_ref[...])
    b = _dot(h, wu_ref[...])
    act = (a * jax.nn.sigmoid(a) * b).astype(BF16)
    o_ref[...] += _dot(act, wd_ref[...])

    @pl.when(j == pl.num_programs(1) - 1)
    def _():
        o_ref[...] = x_ref[...] + 0.5 * o_ref[...]


def _ffn(x, gain, w_gate, w_up, w_down, *, tm, tf):
    t, d = x.shape
    f = w_gate.shape[1]
    return pl.pallas_call(
        _ffn_kernel,
        out_shape=jax.ShapeDtypeStruct((t, d), F32),
        grid=(t // tm, f // tf),
        in_specs=[
            pl.BlockSpec((tm, d), lambda i, j: (i, 0)),
            pl.BlockSpec((1, d), lambda i, j: (0, 0)),
            pl.BlockSpec((d, tf), lambda i, j: (0, j)),
            pl.BlockSpec((d, tf), lambda i, j: (0, j)),
            pl.BlockSpec((tf, d), lambda i, j: (j, 0)),
        ],
        out_specs=pl.BlockSpec((tm, d), lambda i, j: (i, 0)),
        scratch_shapes=[pltpu.VMEM((tm, d), BF16)],
        compiler_params=_cparams(("parallel", "arbitrary")),
        name="ffn",
    )(x, gain, w_gate, w_up, w_down)


def _norm_matmul_kernel(x_ref, g_ref, w_ref, o_ref, h_ref):
    @pl.when(pl.program_id(1) == 0)
    def _():
        h_ref[...] = _rms(x_ref[...], g_ref[...], NORM_EPS).astype(BF16)

    o_ref[...] = _dot(h_ref[...], w_ref[...]).astype(o_ref.dtype)


def _norm_matmul(x, gain, w, *, tm, tn):
    t, d = x.shape
    n = w.shape[1]
    return pl.pallas_call(
        _norm_matmul_kernel,
        out_shape=jax.ShapeDtypeStruct((t, n), F32),
        grid=(t // tm, n // tn),
        in_specs=[
            pl.BlockSpec((tm, d), lambda i, j: (i, 0)),
            pl.BlockSpec((1, d), lambda i, j: (0, 0)),
            pl.BlockSpec((d, tn), lambda i, j: (0, j)),
        ],
        out_specs=pl.BlockSpec((tm, tn), lambda i, j: (i, j)),
        scratch_shapes=[pltpu.VMEM((tm, d), BF16)],
        compiler_params=_cparams(("parallel", "arbitrary")),
        name="norm_matmul",
    )(x, gain, w)


def _rope(y, cos, sin_signed):
    lane = lax.broadcasted_iota(jnp.int32, cos.shape, 1)
    half = ROT_DIM // 2
    outs = []
    for c in range(y.shape[1] // LANES):
        t = y[:, c * LANES:(c + 1) * LANES]
        swapped = jnp.where(lane < half, pltpu.roll(t, LANES - half, axis=1),
                            pltpu.roll(t, half, axis=1))
        outs.append(t * cos + swapped * sin_signed)
    return jnp.concatenate(outs, axis=1)


def _ab_proj_kernel(x_ref, g_ref, w_ref, pos_ref, freq_ref, sign_ref, u_ref, qkv_ref,
                    h_ref, cos_ref, sin_ref, *, q_scale):
    j = pl.program_id(1)

    @pl.when(j == 0)
    def _():
        h_ref[...] = _rms(x_ref[...], g_ref[...], NORM_EPS).astype(BF16)
        ang = pos_ref[...] * freq_ref[...]
        cos_ref[...] = jnp.cos(ang)
        sin_ref[...] = jnp.sin(ang) * sign_ref[...]

    y = _dot(h_ref[...], w_ref[...])

    @pl.when(j == 0)
    def _():
        u_ref[...] = y

    @pl.when(j == 1)
    def _():
        qkv_ref[...] = (_rope(y, cos_ref[...], sin_ref[...]) * q_scale).astype(BF16)

    @pl.when(j == 2)
    def _():
        qkv_ref[...] = _rope(y, cos_ref[...], sin_ref[...]).astype(BF16)

    @pl.when(j == 3)
    def _():
        qkv_ref[...] = y.astype(BF16)


def _ab_proj(x, gain, w, pos, freq, sign, *, tm):
    t, d = x.shape
    n = w.shape[1]
    tn = n // 4
    return pl.pallas_call(
        functools.partial(_ab_proj_kernel, q_scale=DIFF_HEAD_DIM ** -0.5),
        out_shape=(jax.ShapeDtypeStruct((t, tn), F32), jax.ShapeDtypeStruct((t, 3 * tn), BF16)),
        grid=(t // tm, 4),
        in_specs=[
            pl.BlockSpec((tm, d), lambda i, j: (i, 0)),
            pl.BlockSpec((1, d), lambda i, j: (0, 0)),
            pl.BlockSpec((d, tn), lambda i, j: (0, j)),
            pl.BlockSpec((tm, 1), lambda i, j: (i, 0)),
            pl.BlockSpec((1, LANES), lambda i, j: (0, 0)),
            pl.BlockSpec((1, LANES), lambda i, j: (0, 0)),
        ],
        out_specs=(
            pl.BlockSpec((tm, tn), lambda i, j: (i, 0)),
            pl.BlockSpec((tm, tn), lambda i, j: (i, jnp.maximum(j - 1, 0))),
        ),
        scratch_shapes=[pltpu.VMEM((tm, d), BF16), pltpu.VMEM((tm, LANES), F32),
                        pltpu.VMEM((tm, LANES), F32)],
        compiler_params=_cparams(("parallel", "arbitrary")),
        name="ab_proj",
    )(x, gain, w, pos, freq, sign)


def _pool_kernel(u_ref, w_ref, s_ref, o_ref, ext_ref, *, ts):
    i = pl.program_id(1)

    @pl.when(i == 0)
    def _():
        ext_ref[0:POOL_HALO, :] = jnp.zeros((POOL_HALO, ext_ref.shape[1]), F32)

    ext_ref[POOL_HALO:, :] = u_ref[0]
    gd = w_ref.shape[1]
    row = lax.broadcasted_iota(jnp.int32, (ts, gd), 0) + i * ts
    for g, win in enumerate(POOL_WINDOWS):
        cols = slice(g * gd, (g + 1) * gd)
        acc = ext_ref[:, cols]
        shift = 1
        while shift < win:
            acc = acc + pltpu.roll(acc, shift, axis=0)
            shift *= 2
        cnt = jnp.minimum(row + 1, win).astype(F32)
        v = u_ref[0, :, cols]
        dlt = acc[POOL_HALO:, :] / cnt - v
        y = _dot(dlt.astype(BF16), w_ref[g]) * s_ref[:, cols]
        o_ref[0, :, cols] = y.astype(o_ref.dtype)
    ext_ref[0:POOL_HALO, :] = ext_ref[ts:ts + POOL_HALO, :]


def _pool(u3, pool_w, pool_scale, *, ts):
    b, s, pw = u3.shape
    g, gd, _ = pool_w.shape
    return pl.pallas_call(
        functools.partial(_pool_kernel, ts=ts),
        out_shape=jax.ShapeDtypeStruct((b, s, pw), BF16),
        grid=(b, s // ts),
        in_specs=[
            pl.BlockSpec((1, ts, pw), lambda bi, i: (bi, i, 0)),
            pl.BlockSpec((g, gd, gd), lambda bi, i: (0, 0, 0)),
            pl.BlockSpec((1, pw), lambda bi, i: (0, 0)),
        ],
        out_specs=pl.BlockSpec((1, ts, pw), lambda bi, i: (bi, i, 0)),
        scratch_shapes=[pltpu.VMEM((ts + POOL_HALO, pw), F32)],
        compiler_params=_cparams(("parallel", "arbitrary")),
        name="pool",
    )(u3, pool_w, pool_scale)


def _attn_kernel(lam_ref, q_ref, k_ref, v_ref, sg_ref, o_ref, m_ref, l_ref, acc_ref, *,
                 tq, lam_init):
    dh = DIFF_HEAD_DIM
    qi = pl.program_id(2)
    q = q_ref[0]
    m_ref[...] = jnp.full(m_ref.shape, -jnp.inf, F32)
    l_ref[...] = jnp.zeros(l_ref.shape, F32)
    acc_ref[...] = jnp.zeros(acc_ref.shape, F32)

    def step(kb, masked):
        r0 = pl.multiple_of(kb * tq, tq)
        k = k_ref[0, pl.ds(r0, tq), :]
        v = v_ref[0, pl.ds(r0, tq), :]
        for c in range(2):
            s = _dot_nt(q[:, c * dh:(c + 1) * dh], k[:, c * dh:(c + 1) * dh])
            if masked:
                row = lax.broadcasted_iota(jnp.int32, s.shape, 0)
                col = lax.broadcasted_iota(jnp.int32, s.shape, 1)
                s = jnp.where(col <= row, s, -jnp.inf)
            m_prev = m_ref[c]
            m_new = jnp.maximum(m_prev, jnp.max(s, axis=-1, keepdims=True))
            alpha = jnp.exp(m_prev - m_new)
            p = jnp.exp(s - m_new)
            l_ref[c] = alpha * l_ref[c] + jnp.sum(p, axis=-1, keepdims=True)
            acc_ref[c] = alpha * acc_ref[c] + _dot(p.astype(BF16), v)
            m_ref[c] = m_new

    def body(kb, carry):
        step(kb, False)
        return carry

    lax.fori_loop(0, qi, body, 0)
    step(qi, True)

    lp = lam_ref[...]
    lam = (jnp.exp(jnp.sum(lp[0:1] * lp[1:2], axis=-1, keepdims=True))
           - jnp.exp(jnp.sum(lp[2:3] * lp[3:4], axis=-1, keepdims=True)) + lam_init)
    o = acc_ref[0] / l_ref[0] - lam * (acc_ref[1] / l_ref[1])
    o = _rms(o, sg_ref[...], SUBLN_EPS) * (1.0 - lam_init)
    o_ref[0] = o.astype(o_ref.dtype)


def _diff_attention(qkv3, lam_params, subln, lam_init, *, heads, tq):
    b, s, w3 = qkv3.shape
    hw = 2 * DIFF_HEAD_DIM
    return pl.pallas_call(
        functools.partial(_attn_kernel, tq=tq, lam_init=lam_init),
        out_shape=jax.ShapeDtypeStruct((b, s, w3 // 3), BF16),
        grid=(b, heads, s // tq),
        in_specs=[
            pl.BlockSpec((4, DIFF_HEAD_DIM), lambda bi, h, i: (0, 0)),
            pl.BlockSpec((1, tq, hw), lambda bi, h, i: (bi, i, h)),
            pl.BlockSpec((1, s, hw), lambda bi, h, i: (bi, 0, heads + h)),
            pl.BlockSpec((1, s, hw), lambda bi, h, i: (bi, 0, 2 * heads + h)),
            pl.BlockSpec((1, hw), lambda bi, h, i: (0, 0)),
        ],
        out_specs=pl.BlockSpec((1, tq, hw), lambda bi, h, i: (bi, i, h)),
        scratch_shapes=[pltpu.VMEM((2, tq, 1), F32), pltpu.VMEM((2, tq, 1), F32),
                        pltpu.VMEM((2, tq, hw), F32)],
        compiler_params=_cparams(("parallel", "parallel", "arbitrary")),
        name="diff_attention",
    )(lam_params, qkv3, qkv3, qkv3, subln)


def _out_proj_kernel(*refs):
    x_ref, w_ref, o_ref = refs[0], refs[-2], refs[-1]
    acc = x_ref[...]
    k0 = 0
    for a_ref in refs[1:-2]:
        kw = a_ref.shape[1]
        acc = acc + _dot(a_ref[...], w_ref[k0:k0 + kw, :])
        k0 += kw
    o_ref[...] = acc


def _out_proj(x, parts, w, *, tm):
    t, d = x.shape
    in_specs = [pl.BlockSpec((tm, d), lambda i: (i, 0))]
    in_specs += [pl.BlockSpec((tm, a.shape[1]), lambda i: (i, 0)) for a in parts]
    in_specs += [pl.BlockSpec(w.shape, lambda i: (0, 0))]
    return pl.pallas_call(
        _out_proj_kernel,
        out_shape=jax.ShapeDtypeStruct((t, d), F32),
        grid=(t // tm,),
        in_specs=in_specs,
        out_specs=pl.BlockSpec((tm, d), lambda i: (i, 0)),
        compiler_params=_cparams(("parallel",)),
        name="out_proj",
    )(x, *parts, w)


def _ple_kernel(x_ref, p_ref, g_ref, wg_ref, wp_ref, fg_ref, o_ref, *, final_norm):
    x = x_ref[...]
    h = _rms(x, g_ref[...], NORM_EPS).astype(BF16)
    gate = jax.nn.sigmoid(_dot(h, wg_ref[...]))
    y = x + _dot(p_ref[...].astype(BF16), wp_ref[...]) * gate
    if final_norm:
        y = _rms(y, fg_ref[...], NORM_EPS)
    o_ref[...] = y


def _ple(x, p, gain, w_gate, w_proj, final_gain, *, tm, final_norm):
    t, d = x.shape
    pd = p.shape[1]
    return pl.pallas_call(
        functools.partial(_ple_kernel, final_norm=final_norm),
        out_shape=jax.ShapeDtypeStruct((t, d), F32),
        grid=(t // tm,),
        in_specs=[
            pl.BlockSpec((tm, d), lambda i: (i, 0)),
            pl.BlockSpec((tm, pd), lambda i: (i, 0)),
            pl.BlockSpec((1, d), lambda i: (0, 0)),
            pl.BlockSpec((d, d), lambda i: (0, 0)),
            pl.BlockSpec((pd, d), lambda i: (0, 0)),
            pl.BlockSpec((1, d), lambda i: (0, 0)),
        ],
        out_specs=pl.BlockSpec((tm, d), lambda i: (i, 0)),
        compiler_params=_cparams(("parallel",)),
        name="ple",
    )(x, p, gain, w_gate, w_proj, final_gain)


def _hgrn_kernel(q_ref, f_ref, v_ref, g_ref, lb_ref, ng_ref, o_ref, st_ref, qb_ref, kb_ref,
                 gb_ref, ob_ref, *, tt, hb):
    c16 = HGRN_CHUNK
    hd = HGRN_HEAD_DIM
    width = hb * hd

    @pl.when(pl.program_id(2) == 0)
    def _():
        st_ref[...] = jnp.zeros(st_ref.shape, F32)

    q = q_ref[0]
    qb_ref[...] = q * jax.nn.sigmoid(q)
    ff = f_ref[0]
    lb = lb_ref[...]
    log_lb = jnp.log(lb)
    log_sig = jnp.minimum(ff, 0.0) - jnp.log1p(jnp.exp(-jnp.abs(ff)))
    b = jnp.log1p(-lb) + log_sig
    log_f = jnp.maximum(log_lb, b) + jnp.log1p(jnp.exp(-jnp.abs(log_lb - b)))
    kb_ref[...] = (1.0 - lb) * jax.nn.sigmoid(-ff)
    row = lax.broadcasted_iota(jnp.int32, (tt, width), 0) & (c16 - 1)
    cum = log_f
    shift = 1
    while shift < c16:
        cum = cum + jnp.where(row >= shift, pltpu.roll(cum, shift, axis=0), 0.0)
        shift *= 2
    gb_ref[...] = cum

    ones = jnp.ones((hd, hd), BF16)
    trow = lax.broadcasted_iota(jnp.int32, (c16, hd), 0)

    def chunk(c, carry):
        r0 = pl.multiple_of(c * c16, c16)
        for h in range(hb):
            cols = slice(h * hd, (h + 1) * hd)
            gcum = gb_ref[pl.ds(r0, c16), cols]
            qc = qb_ref[pl.ds(r0, c16), cols]
            kc = kb_ref[pl.ds(r0, c16), cols]
            vc = v_ref[0, pl.ds(r0, c16), cols]
            glast = gcum[c16 - 1:c16, :]
            state = st_ref[h]
            o = _dot_nt((qc * jnp.exp(gcum)).astype(BF16), state.astype(BF16))
            pairs = []
            for s in range(c16):
                gs = gcum[s:s + 1, :]
                ks = kc[s:s + 1, :]
                e = jnp.exp(jnp.where(trow >= s, gcum - gs, -jnp.inf))
                pairs.append((qc * e * ks).astype(BF16))
            red = _dot(jnp.concatenate(pairs, axis=0), ones)
            for s in range(c16):
                vs = vc[s:s + 1, :]
                o = o + red[s * c16:(s + 1) * c16, :] * vs
            ob_ref[pl.ds(r0, c16), cols] = o
            kdec = (kc * jnp.exp(glast - gcum)).astype(BF16)
            st_ref[h] = state * jnp.exp(glast) + _dot_tn(vc.astype(BF16), kdec)
        return carry

    lax.fori_loop(0, tt // c16, chunk, 0)

    for h in range(hb):
        cols = slice(h * hd, (h + 1) * hd)
        gate = g_ref[0, :, cols]
        y = _rms(ob_ref[:, cols], ng_ref[...], NORM_EPS) * (gate * jax.nn.sigmoid(gate))
        o_ref[0, :, cols] = y.astype(o_ref.dtype)


def _hgrn_scan(proj3, lb, norm_gain, *, heads, tt, hb):
    b, s, w4 = proj3.shape
    hd = HGRN_HEAD_DIM
    width = hb * hd
    ng = heads // hb

    def sec(k):
        return pl.BlockSpec((1, tt, width), lambda bi, hg, i: (bi, i, k * ng + hg))

    return pl.pallas_call(
        functools.partial(_hgrn_kernel, tt=tt, hb=hb),
        out_shape=jax.ShapeDtypeStruct((b, s, w4 // 4), BF16),
        grid=(b, ng, s // tt),
        in_specs=[sec(0), sec(1), sec(2), sec(3),
                  pl.BlockSpec((1, width), lambda bi, hg, i: (0, hg)),
                  pl.BlockSpec((1, hd), lambda bi, hg, i: (0, 0))],
        out_specs=pl.BlockSpec((1, tt, width), lambda bi, hg, i: (bi, i, hg)),
        scratch_shapes=[pltpu.VMEM((hb, hd, hd), F32), pltpu.VMEM((tt, width), F32),
                        pltpu.VMEM((tt, width), F32), pltpu.VMEM((tt, width), F32),
                        pltpu.VMEM((tt, width), F32)],
        compiler_params=_cparams(("parallel", "parallel", "arbitrary")),
        name="hgrn_scan",
    )(proj3, proj3, proj3, proj3, lb, norm_gain)


def kernel(x, p, positions, ffn1_norm, ffn1_w_gate, ffn1_w_up, ffn1_w_down, mix_norm, ffn2_norm, ffn2_w_gate, ffn2_w_up, ffn2_w_down, ple_norm, ple_w_gate, ple_w_proj, ab_w_in, pool_w, pool_scale, diff_lambda, diff_subln, ab_w_out, hgrn_w_in, hgrn_lower_bounds, hgrn_norm, hgrn_w_out, final_norm):
    bsz, seq, d = x.shape
    depth = p.shape[0]
    t = bsz * seq
    tm_ffn = min(1024, t)
    tf = 512
    tm = min(512, t)
    ts = min(512, seq)

    bf = lambda a: a.astype(BF16)
    row = lambda a: a.reshape(1, -1).astype(F32)

    half = ROT_DIM // 2
    inv_freq = ROPE_THETA ** (-jnp.arange(0, ROT_DIM, 2, dtype=F32) / ROT_DIM)
    zeros = jnp.zeros((LANES - ROT_DIM,), F32)
    freq = jnp.concatenate([inv_freq, inv_freq, zeros]).reshape(1, LANES)
    sign = jnp.concatenate([-jnp.ones((half,), F32), jnp.ones((half,), F32), zeros]).reshape(1, LANES)
    pos = positions.reshape(t, 1).astype(F32)

    lbs = jax.nn.softmax(hgrn_lower_bounds.astype(F32), axis=0)
    lbs = jnp.cumsum(lbs, axis=0) - lbs[0]

    xt = x.reshape(t, d)
    for i in range(depth):
        xt = _ffn(xt, row(ffn1_norm[i]), bf(ffn1_w_gate[i]), bf(ffn1_w_up[i]), bf(ffn1_w_down[i]),
                  tm=tm_ffn, tf=tf)
        if i % 2 == 0:
            e = i // 2
            lam_init = 0.8 - 0.6 * math.exp(-0.3 * i)
            u, qkv = _ab_proj(xt, row(mix_norm[i]), bf(ab_w_in[e]), pos, freq, sign, tm=tm_ffn)
            pw = u.shape[1]
            a_out = _pool(u.reshape(bsz, seq, pw), bf(pool_w[e]), row(pool_scale[e]), ts=ts)
            heads = qkv.shape[1] // (3 * 2 * DIFF_HEAD_DIM)
            b_out = _diff_attention(qkv.reshape(bsz, seq, -1), diff_lambda[e].astype(F32),
                                    row(diff_subln[e]), lam_init, heads=heads, tq=ts)
            xt = _out_proj(xt, [a_out.reshape(t, -1), b_out.reshape(t, -1)], bf(ab_w_out[e]), tm=tm)
        else:
            o = i // 2
            proj = _norm_matmul(xt, row(mix_norm[i]), bf(hgrn_w_in[o]), tm=tm_ffn, tn=1024)
            heads = d // HGRN_HEAD_DIM
            mixed = _hgrn_scan(proj.reshape(bsz, seq, -1), row(lbs[i]), row(hgrn_norm[o]),
                               heads=heads, tt=ts, hb=2)
            xt = _out_proj(xt, [mixed.reshape(t, -1)], bf(hgrn_w_out[o]), tm=tm)
        xt = _ffn(xt, row(ffn2_norm[i]), bf(ffn2_w_gate[i]), bf(ffn2_w_up[i]), bf(ffn2_w_down[i]),
                  tm=tm_ffn, tf=tf)
        xt = _ple(xt, p[i].reshape(t, -1), row(ple_norm[i]), bf(ple_w_gate[i]), bf(ple_w_proj[i]),
                  row(final_norm), tm=tm, final_norm=(i == depth - 1))
    return xt.reshape(bsz, seq, d)
```

```python
import functools
import math

import jax
import jax.numpy as jnp
from jax import lax
from jax.experimental import pallas as pl
from jax.experimental.pallas import tpu as pltpu

F32 = jnp.float32
BF16 = jnp.bfloat16

NORM_EPS = 1e-6
SUBLN_EPS = 1e-5
POOL_WINDOWS = (2, 4, 8, 16)
POOL_HALO = 16
DIFF_HEAD_DIM = 128
ROT_DIM = DIFF_HEAD_DIM // 4
ROPE_THETA = 500000.0
HGRN_HEAD_DIM = 128
HGRN_CHUNK = 16

LANES = 128
VMEM_LIMIT = 56 * 1024 * 1024


def _cparams(sem):
    return pltpu.CompilerParams(dimension_semantics=sem, vmem_limit_bytes=VMEM_LIMIT)


def _rms(x, gain, eps):
    ms = jnp.mean(x * x, axis=-1, keepdims=True)
    return x * lax.rsqrt(ms + eps) * gain


def _dot(a, b):
    return jnp.dot(a, b, preferred_element_type=F32)


def _dot_nt(a, b):
    return lax.dot_general(a, b, (((1,), (1,)), ((), ())), preferred_element_type=F32)


def _dot_tn(a, b):
    return lax.dot_general(a, b, (((0,), (0,)), ((), ())), preferred_element_type=F32)


def _ffn_kernel(x_ref, g_ref, wg_ref, wu_ref, wd_ref, o_ref, h_ref):
    j = pl.program_id(1)

    @pl.when(j == 0)
    def _():
        h_ref[...] = _rms(x_ref[...], g_ref[...], NORM_EPS).astype(BF16)
        o_ref[...] = jnp.zeros_like(o_ref)

    h = h_ref[...]
    a = _dot(h, wg_ref[...])
    b = _dot(h, wu_ref[...])
    act = (a * jax.nn.sigmoid(a) * b).astype(BF16)
    o_ref[...] += _dot(act, wd_ref[...])

    @pl.when(j == pl.num_programs(1) - 1)
    def _():
        o_ref[...] = x_ref[...] + 0.5 * o_ref[...]


def _ffn(x, gain, w_gate, w_up, w_down, *, tm, tf):
    t, d = x.shape
    f = w_gate.shape[1]
    return pl.pallas_call(
        _ffn_kernel,
        out_shape=jax.ShapeDtypeStruct((t, d), F32),
        grid=(t // tm, f // tf),
        in_specs=[
            pl.BlockSpec((tm, d), lambda i, j: (i, 0)),
            pl.BlockSpec((1, d), lambda i, j: (0, 0)),
            pl.BlockSpec((d, tf), lambda i, j: (0, j)),
            pl.BlockSpec((d, tf), lambda i, j: (0, j)),
            pl.BlockSpec((tf, d), lambda i, j: (j, 0)),
        ],
        out_specs=pl.BlockSpec((tm, d), lambda i, j: (i, 0)),
        scratch_shapes=[pltpu.VMEM((tm, d), BF16)],
        compiler_params=_cparams(("parallel", "arbitrary")),
        name="ffn",
    )(x, gain, w_gate, w_up, w_down)


def _norm_matmul_kernel(x_ref, g_ref, w_ref, o_ref, h_ref):
    @pl.when(pl.program_id(1) == 0)
    def _():
        h_ref[...] = _rms(x_ref[...], g_ref[...], NORM_EPS).astype(BF16)

    o_ref[...] = _dot(h_ref[...], w_ref[...]).astype(o_ref.dtype)


def _norm_matmul(x, gain, w, *, tm, tn):
    t, d = x.shape
    n = w.shape[1]
    return pl.pallas_call(
        _norm_matmul_kernel,
        out_shape=jax.ShapeDtypeStruct((t, n), F32),
        grid=(t // tm, n // tn),
        in_specs=[
            pl.BlockSpec((tm, d), lambda i, j: (i, 0)),
            pl.BlockSpec((1, d), lambda i, j: (0, 0)),
            pl.BlockSpec((d, tn), lambda i, j: (0, j)),
        ],
        out_specs=pl.BlockSpec((tm, tn), lambda i, j: (i, j)),
        scratch_shapes=[pltpu.VMEM((tm, d), BF16)],
        compiler_params=_cparams(("parallel", "arbitrary")),
        name="norm_matmul",
    )(x, gain, w)


def _rope(y, cos, sin_signed):
    lane = lax.broadcasted_iota(jnp.int32, cos.shape, 1)
    half = ROT_DIM // 2
    outs = []
    for c in range(y.shape[1] // LANES):
        t = y[:, c * LANES:(c + 1) * LANES]
        swapped = jnp.where(lane < half, pltpu.roll(t, LANES - half, axis=1),
                            pltpu.roll(t, half, axis=1))
        outs.append(t * cos + swapped * sin_signed)
    return jnp.concatenate(outs, axis=1)


def _ab_proj_kernel(x_ref, g_ref, w_ref, pos_ref, freq_ref, sign_ref, u_ref, qkv_ref,
                    h_ref, cos_ref, sin_ref, *, q_scale):
    j = pl.program_id(1)

    @pl.when(j == 0)
    def _():
        h_ref[...] = _rms(x_ref[...], g_ref[...], NORM_EPS).astype(BF16)
        ang = pos_ref[...] * freq_ref[...]
        cos_ref[...] = jnp.cos(ang)
        sin_ref[...] = jnp.sin(ang) * sign_ref[...]

    y = _dot(h_ref[...], w_ref[...])

    @pl.when(j == 0)
    def _():
        u_ref[...] = y

    @pl.when(j == 1)
    def _():
        qkv_ref[...] = (_rope(y, cos_ref[...], sin_ref[...]) * q_scale).astype(BF16)

    @pl.when(j == 2)
    def _():
        qkv_ref[...] = _rope(y, cos_ref[...], sin_ref[...]).astype(BF16)

    @pl.when(j == 3)
    def _():
        qkv_ref[...] = y.astype(BF16)


def _ab_proj(x, gain, w, pos, freq, sign, *, tm):
    t, d = x.shape
    n = w.shape[1]
    tn = n // 4
    return pl.pallas_call(
        functools.partial(_ab_proj_kernel, q_scale=DIFF_HEAD_DIM ** -0.5),
        out_shape=(jax.ShapeDtypeStruct((t, tn), F32), jax.ShapeDtypeStruct((t, 3 * tn), BF16)),
        grid=(t // tm, 4),
        in_specs=[
            pl.BlockSpec((tm, d), lambda i, j: (i, 0)),
            pl.BlockSpec((1, d), lambda i, j: (0, 0)),
            pl.BlockSpec((d, tn), lambda i, j: (0, j)),
            pl.BlockSpec((tm, 1), lambda i, j: (i, 0)),
            pl.BlockSpec((1, LANES), lambda i, j: (0, 0)),
            pl.BlockSpec((1, LANES), lambda i, j: (0, 0)),
        ],
        out_specs=(
            pl.BlockSpec((tm, tn), lambda i, j: (i, 0)),
            pl.BlockSpec((tm, tn), lambda i, j: (i, jnp.maximum(j - 1, 0))),
        ),
        scratch_shapes=[pltpu.VMEM((tm, d), BF16), pltpu.VMEM((tm, LANES), F32),
                        pltpu.VMEM((tm, LANES), F32)],
        compiler_params=_cparams(("parallel", "arbitrary")),
        name="ab_proj",
    )(x, gain, w, pos, freq, sign)


def _pool_kernel(u_ref, w_ref, s_ref, o_ref, ext_ref, *, ts):
    i = pl.program_id(1)

    @pl.when(i == 0)
    def _():
        ext_ref[0:POOL_HALO, :] = jnp.zeros((POOL_HALO, ext_ref.shape[1]), F32)

    ext_ref[POOL_HALO:, :] = u_ref[0]
    gd = w_ref.shape[1]
    row = lax.broadcasted_iota(jnp.int32, (ts, gd), 0) + i * ts
    for g, win in enumerate(POOL_WINDOWS):
        cols = slice(g * gd, (g + 1) * gd)
        acc = ext_ref[:, cols]
        shift = 1
        while shift < win:
            acc = acc + pltpu.roll(acc, shift, axis=0)
            shift *= 2
        cnt = jnp.minimum(row + 1, win).astype(F32)
        v = u_ref[0, :, cols]
        dlt = acc[POOL_HALO:, :] / cnt - v
        y = _dot(dlt.astype(BF16), w_ref[g]) * s_ref[:, cols]
        o_ref[0, :, cols] = y.astype(o_ref.dtype)
    ext_ref[0:POOL_HALO, :] = ext_ref[ts:ts + POOL_HALO, :]


def _pool(u3, pool_w, pool_scale, *, ts):
    b, s, pw = u3.shape
    g, gd, _ = pool_w.shape
    return pl.pallas_call(
        functools.partial(_pool_kernel, ts=ts),
        out_shape=jax.ShapeDtypeStruct((b, s, pw), BF16),
        grid=(b, s // ts),
        in_specs=[
            pl.BlockSpec((1, ts, pw), lambda bi, i: (bi, i, 0)),
            pl.BlockSpec((g, gd, gd), lambda bi, i: (0, 0, 0)),
            pl.BlockSpec((1, pw), lambda bi, i: (0, 0)),
        ],
        out_specs=pl.BlockSpec((1, ts, pw), lambda bi, i: (bi, i, 0)),
        scratch_shapes=[pltpu.VMEM((ts + POOL_HALO, pw), F32)],
        compiler_params=_cparams(("parallel", "arbitrary")),
        name="pool",
    )(u3, pool_w, pool_scale)


def _attn_kernel(lam_ref, q_ref, k_ref, v_ref, sg_ref, o_ref, m_ref, l_ref, acc_ref, *,
                 tq, lam_init):
    dh = DIFF_HEAD_DIM
    qi = pl.program_id(2)
    q = q_ref[0]
    m_ref[...] = jnp.full(m_ref.shape, -jnp.inf, F32)
    l_ref[...] = jnp.zeros(l_ref.shape, F32)
    acc_ref[...] = jnp.zeros(acc_ref.shape, F32)

    def step(kb, masked):
        r0 = pl.multiple_of(kb * tq, tq)
        k = k_ref[0, pl.ds(r0, tq), :]
        v = v_ref[0, pl.ds(r0, tq), :]
        for c in range(2):
            s = _dot_nt(q[:, c * dh:(c + 1) * dh], k[:, c * dh:(c + 1) * dh])
            if masked:
                row = lax.broadcasted_iota(jnp.int32, s.shape, 0)
                col = lax.broadcasted_iota(jnp.int32, s.shape, 1)
                s = jnp.where(col <= row, s, -jnp.inf)
            m_prev = m_ref[c]
            m_new = jnp.maximum(m_prev, jnp.max(s, axis=-1, keepdims=True))
            alpha = jnp.exp(m_prev - m_new)
            p = jnp.exp(s - m_new)
            l_ref[c] = alpha * l_ref[c] + jnp.sum(p, axis=-1, keepdims=True)
            acc_ref[c] = alpha * acc_ref[c] + _dot(p.astype(BF16), v)
            m_ref[c] = m_new

    def body(kb, carry):
        step(kb, False)
        return carry

    lax.fori_loop(0, qi, body, 0)
    step(qi, True)

    lp = lam_ref[...]
    lam = (jnp.exp(jnp.sum(lp[0:1] * lp[1:2], axis=-1, keepdims=True))
           - jnp.exp(jnp.sum(lp[2:3] * lp[3:4], axis=-1, keepdims=True)) + lam_init)
    o = acc_ref[0] / l_ref[0] - lam * (acc_ref[1] / l_ref[1])
    o = _rms(o, sg_ref[...], SUBLN_EPS) * (1.0 - lam_init)
    o_ref[0] = o.astype(o_ref.dtype)


def _diff_attention(qkv3, lam_params, subln, lam_init, *, heads, tq):
    b, s, w3 = qkv3.shape
    hw = 2 * DIFF_HEAD_DIM
    return pl.pallas_call(
        functools.partial(_attn_kernel, tq=tq, lam_init=lam_init),
        out_shape=jax.ShapeDtypeStruct((b, s, w3 // 3), BF16),
        grid=(b, heads, s // tq),
        in_specs=[
            pl.BlockSpec((4, DIFF_HEAD_DIM), lambda bi, h, i: (0, 0)),
            pl.BlockSpec((1, tq, hw), lambda bi, h, i: (bi, i, h)),
            pl.BlockSpec((1, s, hw), lambda bi, h, i: (bi, 0, heads + h)),
            pl.BlockSpec((1, s, hw), lambda bi, h, i: (bi, 0, 2 * heads + h)),
            pl.BlockSpec((1, hw), lambda bi, h, i: (0, 0)),
        ],
        out_specs=pl.BlockSpec((1, tq, hw), lambda bi, h, i: (bi, i, h)),
        scratch_shapes=[pltpu.VMEM((2, tq, 1), F32), pltpu.VMEM((2, tq, 1), F32),
                        pltpu.VMEM((2, tq, hw), F32)],
        compiler_params=_cparams(("parallel", "parallel", "arbitrary")),
        name="diff_attention",
    )(lam_params, qkv3, qkv3, qkv3, subln)


def _out_proj_kernel(*refs):
    x_ref, w_ref, o_ref = refs[0], refs[-2], refs[-1]
    acc = x_ref[...]
    k0 = 0
    for a_ref in refs[1:-2]:
        kw = a_ref.shape[1]
        acc = acc + _dot(a_ref[...], w_ref[k0:k0 + kw, :])
        k0 += kw
    o_ref[...] = acc


def _out_proj(x, parts, w, *, tm):
    t, d = x.shape
    in_specs = [pl.BlockSpec((tm, d), lambda i: (i, 0))]
    in_specs += [pl.BlockSpec((tm, a.shape[1]), lambda i: (i, 0)) for a in parts]
    in_specs += [pl.BlockSpec(w.shape, lambda i: (0, 0))]
    return pl.pallas_call(
        _out_proj_kernel,
        out_shape=jax.ShapeDtypeStruct((t, d), F32),
        grid=(t // tm,),
        in_specs=in_specs,
        out_specs=pl.BlockSpec((tm, d), lambda i: (i, 0)),
        compiler_params=_cparams(("parallel",)),
        name="out_proj",
    )(x, *parts, w)


def _ple_kernel(x_ref, p_ref, g_ref, wg_ref, wp_ref, fg_ref, o_ref, *, final_norm):
    x = x_ref[...]
    h = _rms(x, g_ref[...], NORM_EPS).astype(BF16)
    gate = jax.nn.sigmoid(_dot(h, wg_ref[...]))
    y = x + _dot(p_ref[...].astype(BF16), wp_ref[...]) * gate
    if final_norm:
        y = _rms(y, fg_ref[...], NORM_EPS)
    o_ref[...] = y


def _ple(x, p, gain, w_gate, w_proj, final_gain, *, tm, final_norm):
    t, d = x.shape
    pd = p.shape[1]
    return pl.pallas_call(
        functools.partial(_ple_kernel, final_norm=final_norm),
        out_shape=jax.ShapeDtypeStruct((t, d), F32),
        grid=(t // tm,),
        in_specs=[
            pl.BlockSpec((tm, d), lambda i: (i, 0)),
            pl.BlockSpec((tm, pd), lambda i: (i, 0)),
            pl.BlockSpec((1, d), lambda i: (0, 0)),
            pl.BlockSpec((d, d), lambda i: (0, 0)),
            pl.BlockSpec((pd, d), lambda i: (0, 0)),
            pl.BlockSpec((1, d), lambda i: (0, 0)),
        ],
        out_specs=pl.BlockSpec((tm, d), lambda i: (i, 0)),
        compiler_params=_cparams(("parallel",)),
        name="ple",
    )(x, p, gain, w_gate, w_proj, final_gain)


def _tree_sum(terms):
    while len(terms) > 1:
        terms = [terms[i] + terms[i + 1] for i in range(0, len(terms) - 1, 2)] + (
            [terms[-1]] if len(terms) % 2 else [])
    return terms[0]


def _hgrn_kernel(q_ref, f_ref, v_ref, g_ref, lb_ref, ng_ref, o_ref, st_ref, qb_ref, gb_ref,
                 hb_ref, ob_ref, *, tt, hb):
    c16 = HGRN_CHUNK
    half = c16 // 2
    hd = HGRN_HEAD_DIM
    width = hb * hd
    log2e = math.log2(math.e)

    @pl.when(pl.program_id(2) == 0)
    def _():
        st_ref[...] = jnp.zeros(st_ref.shape, F32)

    q = q_ref[0]
    qb_ref[...] = q * jax.nn.sigmoid(q)
    ff = f_ref[0]
    lb = lb_ref[...]
    log_lb = jnp.log(lb)
    log_sig = jnp.minimum(ff, 0.0) - jnp.log1p(jnp.exp(-jnp.abs(ff)))
    b = jnp.log1p(-lb) + log_sig
    log_f = jnp.maximum(log_lb, b) + jnp.log1p(jnp.exp(-jnp.abs(log_lb - b)))
    log_k = b - ff
    row = lax.broadcasted_iota(jnp.int32, (tt, width), 0) & (c16 - 1)
    cum = log_f
    shift = 1
    while shift < c16:
        cum = cum + jnp.where(row >= shift, pltpu.roll(cum, shift, axis=0), 0.0)
        shift *= 2
    gb_ref[...] = cum * log2e
    hb_ref[...] = (cum - log_k) * log2e

    ones = jnp.ones((hd, hd), BF16)
    t8 = lax.broadcasted_iota(jnp.int32, (half, hd), 0)

    def chunk(c, carry):
        r0 = pl.multiple_of(c * c16, c16)
        for h in range(hb):
            cols = slice(h * hd, (h + 1) * hd)
            g2 = gb_ref[pl.ds(r0, c16), cols]
            h2 = hb_ref[pl.ds(r0, c16), cols]
            qc = qb_ref[pl.ds(r0, c16), cols]
            vc = v_ref[0, pl.ds(r0, c16), cols]
            g_top, g_bot = g2[:half], g2[half:]
            q_top, q_bot = qc[:half], qc[half:]
            lhs = []
            for s in range(half):
                hs = h2[s:s + 1, :]
                d_top = g_top - hs
                if s > 0:
                    d_top = jnp.where(t8 >= s, d_top, -jnp.inf)
                p = jnp.concatenate([q_top * jnp.exp2(d_top), q_bot * jnp.exp2(g_bot - hs)], axis=0)
                lhs.append(p.astype(BF16))
            bots = []
            for s in range(half, c16):
                d_bot = g_bot - h2[s:s + 1, :]
                if s > half:
                    d_bot = jnp.where(t8 >= s - half, d_bot, -jnp.inf)
                bots.append(q_bot * jnp.exp2(d_bot))
            for j in range(0, half, 2):
                lhs.append(jnp.concatenate([bots[j], bots[j + 1]], axis=0).astype(BF16))
            red = _dot(jnp.concatenate(lhs, axis=0), ones)
            top_terms = [red[s * c16:s * c16 + half] * vc[s:s + 1, :] for s in range(half)]
            bot_terms = [red[s * c16 + half:(s + 1) * c16] * vc[s:s + 1, :] for s in range(half)]
            base = half * c16
            bot_terms += [red[base + j * half:base + (j + 1) * half] * vc[half + j:half + j + 1, :]
                          for j in range(half)]
            intra = jnp.concatenate([_tree_sum(top_terms), _tree_sum(bot_terms)], axis=0)
            state = st_ref[h]
            inter = _dot_nt((qc * jnp.exp2(g2)).astype(BF16), state.astype(BF16))
            ob_ref[pl.ds(r0, c16), cols] = inter + intra
            g_last = g2[c16 - 1:c16, :]
            kdec = jnp.exp2(g_last - h2).astype(BF16)
            st_ref[h] = state * jnp.exp2(g_last) + _dot_tn(vc.astype(BF16), kdec)
        return carry

    lax.fori_loop(0, tt // c16, chunk, 0)

    for h in range(hb):
        cols = slice(h * hd, (h + 1) * hd)
        gate = g_ref[0, :, cols]
        y = _rms(ob_ref[:, cols], ng_ref[...], NORM_EPS) * (gate * jax.nn.sigmoid(gate))
        o_ref[0, :, cols] = y.astype(o_ref.dtype)


def _hgrn_scan(proj3, lb, norm_gain, *, heads, tt, hb):
    b, s, w4 = proj3.shape
    hd = HGRN_HEAD_DIM
    width = hb * hd
    ng = heads // hb

    def sec(k):
        return pl.BlockSpec((1, tt, width), lambda bi, hg, i: (bi, i, k * ng + hg))

    return pl.pallas_call(
        functools.partial(_hgrn_kernel, tt=tt, hb=hb),
        out_shape=jax.ShapeDtypeStruct((b, s, w4 // 4), BF16),
        grid=(b, ng, s // tt),
        in_specs=[sec(0), sec(1), sec(2), sec(3),
                  pl.BlockSpec((1, width), lambda bi, hg, i: (0, hg)),
                  pl.BlockSpec((1, hd), lambda bi, hg, i: (0, 0))],
        out_specs=pl.BlockSpec((1, tt, width), lambda bi, hg, i: (bi, i, hg)),
        scratch_shapes=[pltpu.VMEM((hb, hd, hd), F32), pltpu.VMEM((tt, width), F32),
                        pltpu.VMEM((tt, width), F32), pltpu.VMEM((tt, width), F32),
                        pltpu.VMEM((tt, width), F32)],
        compiler_params=_cparams(("parallel", "parallel", "arbitrary")),
        name="hgrn_scan",
    )(proj3, proj3, proj3, proj3, lb, norm_gain)


def kernel(x, p, positions, ffn1_norm, ffn1_w_gate, ffn1_w_up, ffn1_w_down, mix_norm, ffn2_norm, ffn2_w_gate, ffn2_w_up, ffn2_w_down, ple_norm, ple_w_gate, ple_w_proj, ab_w_in, pool_w, pool_scale, diff_lambda, diff_subln, ab_w_out, hgrn_w_in, hgrn_lower_bounds, hgrn_norm, hgrn_w_out, final_norm):
    bsz, seq, d = x.shape
    depth = p.shape[0]
    t = bsz * seq
    tm_ffn = min(1024, t)
    tf = 512
    tm = min(512, t)
    ts = min(512, seq)

    bf = lambda a: a.astype(BF16)
    row = lambda a: a.reshape(1, -1).astype(F32)

    half = ROT_DIM // 2
    inv_freq = ROPE_THETA ** (-jnp.arange(0, ROT_DIM, 2, dtype=F32) / ROT_DIM)
    zeros = jnp.zeros((LANES - ROT_DIM,), F32)
    freq = jnp.concatenate([inv_freq, inv_freq, zeros]).reshape(1, LANES)
    sign = jnp.concatenate([-jnp.ones((half,), F32), jnp.ones((half,), F32), zeros]).reshape(1, LANES)
    pos = positions.reshape(t, 1).astype(F32)

    lbs = jax.nn.softmax(hgrn_lower_bounds.astype(F32), axis=0)
    lbs = jnp.cumsum(lbs, axis=0) - lbs[0]

    xt = x.reshape(t, d)
    for i in range(depth):
        xt = _ffn(xt, row(ffn1_norm[i]), bf(ffn1_w_gate[i]), bf(ffn1_w_up[i]), bf(ffn1_w_down[i]),
                  tm=tm_ffn, tf=tf)
        if i % 2 == 0:
            e = i // 2
            lam_init = 0.8 - 0.6 * math.exp(-0.3 * i)
            u, qkv = _ab_proj(xt, row(mix_norm[i]), bf(ab_w_in[e]), pos, freq, sign, tm=tm_ffn)
            pw = u.shape[1]
            a_out = _pool(u.reshape(bsz, seq, pw), bf(pool_w[e]), row(pool_scale[e]), ts=ts)
            heads = qkv.shape[1] // (3 * 2 * DIFF_HEAD_DIM)
            b_out = _diff_attention(qkv.reshape(bsz, seq, -1), diff_lambda[e].astype(F32),
                                    row(diff_subln[e]), lam_init, heads=heads, tq=ts)
            xt = _out_proj(xt, [a_out.reshape(t, -1), b_out.reshape(t, -1)], bf(ab_w_out[e]), tm=tm)
        else:
            o = i // 2
            proj = _norm_matmul(xt, row(mix_norm[i]), bf(hgrn_w_in[o]), tm=tm_ffn, tn=1024)
            heads = d // HGRN_HEAD_DIM
            mixed = _hgrn_scan(proj.reshape(bsz, seq, -1), row(lbs[i]), row(hgrn_norm[o]),
                               heads=heads, tt=ts, hb=8)
            xt = _out_proj(xt, [mixed.reshape(t, -1)], bf(hgrn_w_out[o]), tm=tm)
        xt = _ffn(xt, row(ffn2_norm[i]), bf(ffn2_w_gate[i]), bf(ffn2_w_up[i]), bf(ffn2_w_down[i]),
                  tm=tm_ffn, tf=tf)
        xt = _ple(xt, p[i].reshape(t, -1), row(ple_norm[i]), bf(ple_w_gate[i]), bf(ple_w_proj[i]),
                  row(final_norm), tm=tm, final_norm=(i == depth - 1))
    return xt.reshape(bsz, seq, d)
```

```python
import functools
import math

import jax
import jax.numpy as jnp
from jax import lax
from jax.experimental import pallas as pl
from jax.experimental.pallas import tpu as pltpu

F32 = jnp.float32
BF16 = jnp.bfloat16

NORM_EPS = 1e-6
SUBLN_EPS = 1e-5
POOL_WINDOWS = (2, 4, 8, 16)
POOL_HALO = 16
DIFF_HEAD_DIM = 128
ROT_DIM = DIFF_HEAD_DIM // 4
ROPE_THETA = 500000.0
HGRN_HEAD_DIM = 128
HGRN_CHUNK = 16

LANES = 128
VMEM_LIMIT = 56 * 1024 * 1024


def _cparams(sem):
    return pltpu.CompilerParams(dimension_semantics=sem, vmem_limit_bytes=VMEM_LIMIT)


def _stacked(block, layer, index, **kwargs):
    return pl.BlockSpec((None,) + tuple(block), lambda *g: (layer,) + tuple(index(*g)), **kwargs)


def _rms(x, gain, eps):
    ms = jnp.mean(x * x, axis=-1, keepdims=True)
    return x * lax.rsqrt(ms + eps) * gain


def _dot(a, b):
    return jnp.dot(a, b, preferred_element_type=F32)


def _dot_nt(a, b):
    return lax.dot_general(a, b, (((1,), (1,)), ((), ())), preferred_element_type=F32)


def _dot_tn(a, b):
    return lax.dot_general(a, b, (((0,), (0,)), ((), ())), preferred_element_type=F32)


def _tree_sum(terms):
    while len(terms) > 1:
        terms = [terms[i] + terms[i + 1] for i in range(0, len(terms) - 1, 2)] + (
            [terms[-1]] if len(terms) % 2 else [])
    return terms[0]


def _ffn_kernel(x_ref, g_ref, wg_ref, wu_ref, wd_ref, o_ref, h_ref):
    j = pl.program_id(1)

    @pl.when(j == 0)
    def _():
        h_ref[...] = _rms(x_ref[...], g_ref[...], NORM_EPS).astype(BF16)
        o_ref[...] = jnp.zeros_like(o_ref)

    h = h_ref[...]
    a = _dot(h, wg_ref[...].astype(BF16))
    b = _dot(h, wu_ref[...].astype(BF16))
    act = (a * jax.nn.sigmoid(a) * b).astype(BF16)
    o_ref[...] += _dot(act, wd_ref[...].astype(BF16))

    @pl.when(j == pl.num_programs(1) - 1)
    def _():
        o_ref[...] = x_ref[...] + 0.5 * o_ref[...]


def _ffn(x, gain, w_gate, w_up, w_down, layer, *, tm, tf):
    t, d = x.shape
    f = w_gate.shape[2]
    return pl.pallas_call(
        _ffn_kernel,
        out_shape=jax.ShapeDtypeStruct((t, d), F32),
        grid=(t // tm, f // tf),
        in_specs=[
            pl.BlockSpec((tm, d), lambda i, j: (i, 0)),
            _stacked((1, d), layer, lambda i, j: (0, 0)),
            _stacked((d, tf), layer, lambda i, j: (0, j)),
            _stacked((d, tf), layer, lambda i, j: (0, j)),
            _stacked((tf, d), layer, lambda i, j: (j, 0)),
        ],
        out_specs=pl.BlockSpec((tm, d), lambda i, j: (i, 0)),
        scratch_shapes=[pltpu.VMEM((tm, d), BF16)],
        compiler_params=_cparams(("parallel", "arbitrary")),
        name="ffn",
    )(x, gain, w_gate, w_up, w_down)


def _norm_matmul_kernel(x_ref, g_ref, w_ref, o_ref, h_ref):
    @pl.when(pl.program_id(1) == 0)
    def _():
        h_ref[...] = _rms(x_ref[...], g_ref[...], NORM_EPS).astype(BF16)

    o_ref[...] = _dot(h_ref[...], w_ref[...].astype(BF16)).astype(o_ref.dtype)


def _norm_matmul(x, gain, w, layer, w_layer, *, tm, tn):
    t, d = x.shape
    n = w.shape[2]
    return pl.pallas_call(
        _norm_matmul_kernel,
        out_shape=jax.ShapeDtypeStruct((t, n), F32),
        grid=(t // tm, n // tn),
        in_specs=[
            pl.BlockSpec((tm, d), lambda i, j: (i, 0)),
            _stacked((1, d), layer, lambda i, j: (0, 0)),
            _stacked((d, tn), w_layer, lambda i, j: (0, j)),
        ],
        out_specs=pl.BlockSpec((tm, tn), lambda i, j: (i, j)),
        scratch_shapes=[pltpu.VMEM((tm, d), BF16)],
        compiler_params=_cparams(("parallel", "arbitrary")),
        name="norm_matmul",
    )(x, gain, w)


def _rope(y, cos, sin_signed):
    lane = lax.broadcasted_iota(jnp.int32, cos.shape, 1)
    half = ROT_DIM // 2
    outs = []
    for c in range(y.shape[1] // LANES):
        t = y[:, c * LANES:(c + 1) * LANES]
        swapped = jnp.where(lane < half, pltpu.roll(t, LANES - half, axis=1),
                            pltpu.roll(t, half, axis=1))
        outs.append(t * cos + swapped * sin_signed)
    return jnp.concatenate(outs, axis=1)


def _ab_proj_kernel(x_ref, g_ref, w_ref, pos_ref, freq_ref, sign_ref, u_ref, qkv_ref,
                    h_ref, cos_ref, sin_ref, *, q_scale, per):
    j = pl.program_id(1)
    sec = j // per

    @pl.when(j == 0)
    def _():
        h_ref[...] = _rms(x_ref[...], g_ref[...], NORM_EPS).astype(BF16)
        ang = pos_ref[...] * freq_ref[...]
        cos_ref[...] = jnp.cos(ang)
        sin_ref[...] = jnp.sin(ang) * sign_ref[...]

    y = _dot(h_ref[...], w_ref[...].astype(BF16))

    @pl.when(sec == 0)
    def _():
        u_ref[...] = y

    @pl.when(sec == 1)
    def _():
        qkv_ref[...] = (_rope(y, cos_ref[...], sin_ref[...]) * q_scale).astype(BF16)

    @pl.when(sec == 2)
    def _():
        qkv_ref[...] = _rope(y, cos_ref[...], sin_ref[...]).astype(BF16)

    @pl.when(sec == 3)
    def _():
        qkv_ref[...] = y.astype(BF16)


def _ab_proj(x, gain, w, pos, freq, sign, layer, w_layer, *, tm, tn):
    t, d = x.shape
    n = w.shape[2]
    per = n // 4 // tn
    q_scale = DIFF_HEAD_DIM ** -0.5 * math.log2(math.e)
    return pl.pallas_call(
        functools.partial(_ab_proj_kernel, q_scale=q_scale, per=per),
        out_shape=(jax.ShapeDtypeStruct((t, n // 4), F32),
                   jax.ShapeDtypeStruct((t, 3 * n // 4), BF16)),
        grid=(t // tm, 4 * per),
        in_specs=[
            pl.BlockSpec((tm, d), lambda i, j: (i, 0)),
            _stacked((1, d), layer, lambda i, j: (0, 0)),
            _stacked((d, tn), w_layer, lambda i, j: (0, j)),
            pl.BlockSpec((tm, 1), lambda i, j: (i, 0)),
            pl.BlockSpec((1, LANES), lambda i, j: (0, 0)),
            pl.BlockSpec((1, LANES), lambda i, j: (0, 0)),
        ],
        out_specs=(
            pl.BlockSpec((tm, tn), lambda i, j: (i, jnp.minimum(j, per - 1))),
            pl.BlockSpec((tm, tn), lambda i, j: (i, jnp.maximum(j - per, 0))),
        ),
        scratch_shapes=[pltpu.VMEM((tm, d), BF16), pltpu.VMEM((tm, LANES), F32),
                        pltpu.VMEM((tm, LANES), F32)],
        compiler_params=_cparams(("parallel", "arbitrary")),
        name="ab_proj",
    )(x, gain, w, pos, freq, sign)


def _pool_kernel(u_ref, w_ref, s_ref, o_ref, ext_ref, *, ts):
    i = pl.program_id(1)

    @pl.when(i == 0)
    def _():
        ext_ref[0:POOL_HALO, :] = jnp.zeros((POOL_HALO, ext_ref.shape[1]), F32)

    ext_ref[POOL_HALO:, :] = u_ref[0]
    gd = w_ref.shape[1]
    row = lax.broadcasted_iota(jnp.int32, (ts, gd), 0) + i * ts
    for g, win in enumerate(POOL_WINDOWS):
        cols = slice(g * gd, (g + 1) * gd)
        acc = ext_ref[:, cols]
        shift = 1
        while shift < win:
            acc = acc + pltpu.roll(acc, shift, axis=0)
            shift *= 2
        cnt = jnp.minimum(row + 1, win).astype(F32)
        v = u_ref[0, :, cols]
        dlt = acc[POOL_HALO:, :] / cnt - v
        y = _dot(dlt.astype(BF16), w_ref[g].astype(BF16)) * s_ref[:, cols]
        o_ref[0, :, cols] = y.astype(o_ref.dtype)
    ext_ref[0:POOL_HALO, :] = ext_ref[ts:ts + POOL_HALO, :]


def _pool(u3, pool_w, pool_scale, layer, *, ts):
    b, s, pw = u3.shape
    _, g, gd, _ = pool_w.shape
    return pl.pallas_call(
        functools.partial(_pool_kernel, ts=ts),
        out_shape=jax.ShapeDtypeStruct((b, s, pw), BF16),
        grid=(b, s // ts),
        in_specs=[
            pl.BlockSpec((1, ts, pw), lambda bi, i: (bi, i, 0)),
            _stacked((g, gd, gd), layer, lambda bi, i: (0, 0, 0)),
            _stacked((1, pw), layer, lambda bi, i: (0, 0)),
        ],
        out_specs=pl.BlockSpec((1, ts, pw), lambda bi, i: (bi, i, 0)),
        scratch_shapes=[pltpu.VMEM((ts + POOL_HALO, pw), F32)],
        compiler_params=_cparams(("parallel", "arbitrary")),
        name="pool",
    )(u3, pool_w, pool_scale)


def _attn_kernel(lam_ref, q_ref, k_ref, v_ref, sg_ref, o_ref, m_ref, l_ref, acc_ref, s_ref,
                 p_ref, a_ref, *, tq, rs, lam_init):
    dh = DIFF_HEAD_DIM
    strip = 16
    qi = pl.program_id(2)
    m_ref[...] = jnp.full(m_ref.shape, -jnp.inf, F32)
    l_ref[...] = jnp.zeros(l_ref.shape, F32)
    acc_ref[...] = jnp.zeros(acc_ref.shape, F32)

    def step(kb, masked):
        r0 = pl.multiple_of(kb * tq, tq)

        def width(r):
            return (r + 1) * rs if masked else tq

        def qk(r):
            nk = width(r)
            for c in range(2):
                comp = slice(c * dh, (c + 1) * dh)
                s_ref[r % 2, c, :, :nk] = _dot_nt(q_ref[0, r * rs:(r + 1) * rs, comp],
                                                  k_ref[0, pl.ds(r0, nk), comp])

        def softmax(r):
            nk = width(r)
            for c in range(2):
                for i in range(rs // strip):
                    lo = r * rs + i * strip
                    rows = slice(lo, lo + strip)
                    ns = min(nk, -(-(lo + strip) // LANES) * LANES) if masked else nk
                    s = s_ref[r % 2, c, i * strip:(i + 1) * strip, :ns]
                    if masked:
                        row = lax.broadcasted_iota(jnp.int32, s.shape, 0) + lo
                        col = lax.broadcasted_iota(jnp.int32, s.shape, 1)
                        s = jnp.where(col <= row, s, -jnp.inf)
                    m_prev = m_ref[c, rows]
                    m_new = jnp.maximum(m_prev, jnp.max(s, axis=-1, keepdims=True))
                    alpha = jnp.exp2(m_prev - m_new)
                    p = jnp.exp2(s - m_new)
                    part = _tree_sum([p[:, j * LANES:(j + 1) * LANES] for j in range(ns // LANES)])
                    l_ref[c, rows] = alpha * l_ref[c, rows] + part
                    m_ref[c, rows] = m_new
                    a_ref[r % 2, c, i * strip:(i + 1) * strip] = alpha
                    prow = slice(c * rs + i * strip, c * rs + (i + 1) * strip)
                    p_ref[r % 2, prow, :ns] = p.astype(BF16)
                    if ns < nk:
                        p_ref[r % 2, prow, ns:nk] = jnp.zeros((strip, nk - ns), BF16)

        def pv(r):
            nk = width(r)
            out = _dot(p_ref[r % 2, :, :nk], v_ref[0, pl.ds(r0, nk), :])
            rows = slice(r * rs, (r + 1) * rs)
            acc_ref[0, rows] = a_ref[r % 2, 0] * acc_ref[0, rows] + out[:rs]
            acc_ref[1, rows] = a_ref[r % 2, 1] * acc_ref[1, rows] + out[rs:]

        nr = tq // rs
        qk(0)
        for r in range(nr):
            if r + 1 < nr:
                qk(r + 1)
            softmax(r)
            pv(r)

    def body(kb, carry):
        step(kb, False)
        return carry

    lax.fori_loop(0, qi, body, 0)
    step(qi, True)

    lp = lam_ref[...]
    lam = (jnp.exp(jnp.sum(lp[0:1] * lp[1:2], axis=-1, keepdims=True))
           - jnp.exp(jnp.sum(lp[2:3] * lp[3:4], axis=-1, keepdims=True)) + lam_init)
    l0 = jnp.sum(l_ref[0], axis=-1, keepdims=True)
    l1 = jnp.sum(l_ref[1], axis=-1, keepdims=True)
    o = acc_ref[0] / l0 - lam * (acc_ref[1] / l1)
    o = _rms(o, sg_ref[...], SUBLN_EPS) * (1.0 - lam_init)
    o_ref[0] = o.astype(o_ref.dtype)


def _diff_attention(qkv3, lam_params, subln, lam_init, layer, *, heads, tq, rs):
    b, s, w3 = qkv3.shape
    hw = 2 * DIFF_HEAD_DIM
    return pl.pallas_call(
        functools.partial(_attn_kernel, tq=tq, rs=rs, lam_init=lam_init),
        out_shape=jax.ShapeDtypeStruct((b, s, w3 // 3), BF16),
        grid=(b, heads, s // tq),
        in_specs=[
            _stacked((4, DIFF_HEAD_DIM), layer, lambda bi, h, i: (0, 0)),
            pl.BlockSpec((1, tq, hw), lambda bi, h, i: (bi, i, h)),
            pl.BlockSpec((1, s, hw), lambda bi, h, i: (bi, 0, heads + h),
                         pipeline_mode=pl.Buffered(1)),
            pl.BlockSpec((1, s, hw), lambda bi, h, i: (bi, 0, 2 * heads + h),
                         pipeline_mode=pl.Buffered(1)),
            _stacked((1, hw), layer, lambda bi, h, i: (0, 0)),
        ],
        out_specs=pl.BlockSpec((1, tq, hw), lambda bi, h, i: (bi, i, h)),
        scratch_shapes=[pltpu.VMEM((2, tq, 1), F32), pltpu.VMEM((2, tq, LANES), F32),
                        pltpu.VMEM((2, tq, hw), F32), pltpu.VMEM((2, 2, rs, tq), F32),
                        pltpu.VMEM((2, 2 * rs, tq), BF16), pltpu.VMEM((2, 2, rs, 1), F32)],
        compiler_params=_cparams(("parallel", "parallel", "arbitrary")),
        name="diff_attention",
    )(lam_params, qkv3, qkv3, qkv3, subln)


def _out_proj_kernel(*refs):
    x_ref, w_ref, o_ref = refs[0], refs[-2], refs[-1]
    acc = x_ref[...]
    k0 = 0
    for a_ref in refs[1:-2]:
        kw = a_ref.shape[1]
        acc = acc + _dot(a_ref[...], w_ref[k0:k0 + kw, :].astype(BF16))
        k0 += kw
    o_ref[...] = acc


def _out_proj(x, parts, w, layer, *, tm, tn):
    t, d = x.shape
    k = w.shape[1]
    in_specs = [pl.BlockSpec((tm, tn), lambda i, j: (i, j))]
    in_specs += [pl.BlockSpec((tm, a.shape[1]), lambda i, j: (i, 0)) for a in parts]
    in_specs += [_stacked((k, tn), layer, lambda i, j: (0, j))]
    return pl.pallas_call(
        _out_proj_kernel,
        out_shape=jax.ShapeDtypeStruct((t, d), F32),
        grid=(t // tm, d // tn),
        in_specs=in_specs,
        out_specs=pl.BlockSpec((tm, tn), lambda i, j: (i, j)),
        compiler_params=_cparams(("parallel", "arbitrary")),
        name="out_proj",
    )(x, *parts, w)


def _ple_kernel(x_ref, p_ref, g_ref, wg_ref, wp_ref, fg_ref, o_ref, wgb_ref, wpb_ref, *,
                final_norm):
    @pl.when(pl.program_id(0) == 0)
    def _():
        wgb_ref[...] = wg_ref[...].astype(BF16)
        wpb_ref[...] = wp_ref[...].astype(BF16)

    x = x_ref[...]
    h = _rms(x, g_ref[...], NORM_EPS).astype(BF16)
    gate = jax.nn.sigmoid(_dot(h, wgb_ref[...]))
    y = x + _dot(p_ref[...].astype(BF16), wpb_ref[...]) * gate
    if final_norm:
        y = _rms(y, fg_ref[...], NORM_EPS)
    o_ref[...] = y


def _ple(x, p, gain, w_gate, w_proj, final_gain, layer, *, tm, final_norm):
    t, d = x.shape
    pd = p.shape[2]
    once = pl.Buffered(1)
    return pl.pallas_call(
        functools.partial(_ple_kernel, final_norm=final_norm),
        out_shape=jax.ShapeDtypeStruct((t, d), F32),
        grid=(t // tm,),
        in_specs=[
            pl.BlockSpec((tm, d), lambda i: (i, 0)),
            _stacked((tm, pd), layer, lambda i: (i, 0)),
            _stacked((1, d), layer, lambda i: (0, 0)),
            _stacked((d, d), layer, lambda i: (0, 0), pipeline_mode=once),
            _stacked((pd, d), layer, lambda i: (0, 0), pipeline_mode=once),
            pl.BlockSpec((1, d), lambda i: (0, 0)),
        ],
        out_specs=pl.BlockSpec((tm, d), lambda i: (i, 0)),
        scratch_shapes=[pltpu.VMEM((d, d), BF16), pltpu.VMEM((pd, d), BF16)],
        compiler_params=_cparams(("arbitrary",)),
        name="ple",
    )(x, p, gain, w_gate, w_proj, final_gain)


def _hgrn_kernel(q_ref, f_ref, v_ref, g_ref, lb_ref, ng_ref, o_ref, st_ref, qb_ref, gb_ref,
                 hb_ref, ob_ref, *, tt, hb):
    c16 = HGRN_CHUNK
    half = c16 // 2
    hd = HGRN_HEAD_DIM
    width = hb * hd
    log2e = math.log2(math.e)

    @pl.when(pl.program_id(2) == 0)
    def _():
        st_ref[...] = jnp.zeros(st_ref.shape, F32)

    q = q_ref[0]
    qb_ref[...] = q * jax.nn.sigmoid(q)
    ff = f_ref[0]
    lb = lb_ref[...]
    log_lb = jnp.log(lb)
    log_sig = jnp.minimum(ff, 0.0) - jnp.log1p(jnp.exp(-jnp.abs(ff)))
    b = jnp.log1p(-lb) + log_sig
    log_f = jnp.maximum(log_lb, b) + jnp.log1p(jnp.exp(-jnp.abs(log_lb - b)))
    log_k = b - ff
    row = lax.broadcasted_iota(jnp.int32, (tt, width), 0) & (c16 - 1)
    cum = log_f
    shift = 1
    while shift < c16:
        cum = cum + jnp.where(row >= shift, pltpu.roll(cum, shift, axis=0), 0.0)
        shift *= 2
    gb_ref[...] = cum * log2e
    hb_ref[...] = (cum - log_k) * log2e

    ones = jnp.ones((hd, hd), BF16)
    t8 = lax.broadcasted_iota(jnp.int32, (half, hd), 0)

    def chunk(c, carry):
        r0 = pl.multiple_of(c * c16, c16)
        for h in range(hb):
            cols = slice(h * hd, (h + 1) * hd)
            g2 = gb_ref[pl.ds(r0, c16), cols]
            h2 = hb_ref[pl.ds(r0, c16), cols]
            qc = qb_ref[pl.ds(r0, c16), cols]
            vc = v_ref[0, pl.ds(r0, c16), cols]
            g_top, g_bot = g2[:half], g2[half:]
            q_top, q_bot = qc[:half], qc[half:]
            lhs = []
            for s in range(half):
                hs = h2[s:s + 1, :]
                d_top = g_top - hs
                if s > 0:
                    d_top = jnp.where(t8 >= s, d_top, -jnp.inf)
                p = jnp.concatenate([q_top * jnp.exp2(d_top), q_bot * jnp.exp2(g_bot - hs)], axis=0)
                lhs.append(p.astype(BF16))
            bots = []
            for s in range(half, c16):
                d_bot = g_bot - h2[s:s + 1, :]
                if s > half:
                    d_bot = jnp.where(t8 >= s - half, d_bot, -jnp.inf)
                bots.append(q_bot * jnp.exp2(d_bot))
            for j in range(0, half, 2):
                lhs.append(jnp.concatenate([bots[j], bots[j + 1]], axis=0).astype(BF16))
            red = _dot(jnp.concatenate(lhs, axis=0), ones)
            top_terms = [red[s * c16:s * c16 + half] * vc[s:s + 1, :] for s in range(half)]
            bot_terms = [red[s * c16 + half:(s + 1) * c16] * vc[s:s + 1, :] for s in range(half)]
            base = half * c16
            bot_terms += [red[base + j * half:base + (j + 1) * half] * vc[half + j:half + j + 1, :]
                          for j in range(half)]
            intra = jnp.concatenate([_tree_sum(top_terms), _tree_sum(bot_terms)], axis=0)
            state = st_ref[h]
            inter = _dot_nt((qc * jnp.exp2(g2)).astype(BF16), state.astype(BF16))
            ob_ref[pl.ds(r0, c16), cols] = inter + intra
            g_last = g2[c16 - 1:c16, :]
            kdec = jnp.exp2(g_last - h2).astype(BF16)
            st_ref[h] = state * jnp.exp2(g_last) + _dot_tn(vc.astype(BF16), kdec)
        return carry

    lax.fori_loop(0, tt // c16, chunk, 0)

    for h in range(hb):
        cols = slice(h * hd, (h + 1) * hd)
        gate = g_ref[0, :, cols]
        y = _rms(ob_ref[:, cols], ng_ref[...], NORM_EPS) * (gate * jax.nn.sigmoid(gate))
        o_ref[0, :, cols] = y.astype(o_ref.dtype)


def _hgrn_scan(proj3, lb, norm_gain, layer, *, heads, tt, hb):
    b, s, w4 = proj3.shape
    hd = HGRN_HEAD_DIM
    width = hb * hd
    ng = heads // hb

    def sec(k):
        return pl.BlockSpec((1, tt, width), lambda bi, hg, i: (bi, i, k * ng + hg))

    return pl.pallas_call(
        functools.partial(_hgrn_kernel, tt=tt, hb=hb),
        out_shape=jax.ShapeDtypeStruct((b, s, w4 // 4), BF16),
        grid=(b, ng, s // tt),
        in_specs=[sec(0), sec(1), sec(2), sec(3),
                  pl.BlockSpec((1, width), lambda bi, hg, i: (0, hg)),
                  _stacked((1, hd), layer, lambda bi, hg, i: (0, 0))],
        out_specs=pl.BlockSpec((1, tt, width), lambda bi, hg, i: (bi, i, hg)),
        scratch_shapes=[pltpu.VMEM((hb, hd, hd), F32), pltpu.VMEM((tt, width), F32),
                        pltpu.VMEM((tt, width), F32), pltpu.VMEM((tt, width), F32),
                        pltpu.VMEM((tt, width), F32)],
        compiler_params=_cparams(("parallel", "parallel", "arbitrary")),
        name="hgrn_scan",
    )(proj3, proj3, proj3, proj3, lb, norm_gain)


def kernel(x, p, positions, ffn1_norm, ffn1_w_gate, ffn1_w_up, ffn1_w_down, mix_norm, ffn2_norm, ffn2_w_gate, ffn2_w_up, ffn2_w_down, ple_norm, ple_w_gate, ple_w_proj, ab_w_in, pool_w, pool_scale, diff_lambda, diff_subln, ab_w_out, hgrn_w_in, hgrn_lower_bounds, hgrn_norm, hgrn_w_out, final_norm):
    bsz, seq, d = x.shape
    depth = p.shape[0]
    t = bsz * seq
    tm_big = min(1024, t)
    tm = min(256, t)
    ts = min(512, seq)

    def gains(a):
        return a.reshape(a.shape[0], 1, a.shape[-1]).astype(F32)

    half = ROT_DIM // 2
    inv_freq = ROPE_THETA ** (-jnp.arange(0, ROT_DIM, 2, dtype=F32) / ROT_DIM)
    zeros = jnp.zeros((LANES - ROT_DIM,), F32)
    freq = jnp.concatenate([inv_freq, inv_freq, zeros]).reshape(1, LANES)
    sign = jnp.concatenate([-jnp.ones((half,), F32), jnp.ones((half,), F32), zeros]).reshape(1, LANES)
    pos = positions.reshape(t, 1).astype(F32)

    lbs = jax.nn.softmax(hgrn_lower_bounds.astype(F32), axis=0)
    lbs = jnp.cumsum(lbs, axis=0) - lbs[0]

    ffn1_g, ffn2_g, mix_g, ple_g = gains(ffn1_norm), gains(ffn2_norm), gains(mix_norm), gains(ple_norm)
    p3 = p.reshape(depth, t, p.shape[-1])
    xt = x.reshape(t, d)
    for i in range(depth):
        xt = _ffn(xt, ffn1_g, ffn1_w_gate, ffn1_w_up, ffn1_w_down, i, tm=tm_big, tf=256)
        if i % 2 == 0:
            e = i // 2
            lam_init = 0.8 - 0.6 * math.exp(-0.3 * i)
            u, qkv = _ab_proj(xt, mix_g, ab_w_in, pos, freq, sign, i, e, tm=tm_big, tn=512)
            pw = u.shape[1]
            a_out = _pool(u.reshape(bsz, seq, pw), pool_w, gains(pool_scale), e, ts=ts)
            heads = qkv.shape[1] // (3 * 2 * DIFF_HEAD_DIM)
            b_out = _diff_attention(qkv.reshape(bsz, seq, -1), diff_lambda.astype(F32),
                                    gains(diff_subln), lam_init, e, heads=heads,
                                    tq=min(2048, seq), rs=min(512, seq))
            xt = _out_proj(xt, [a_out.reshape(t, -1), b_out.reshape(t, -1)], ab_w_out, e,
                           tm=tm_big, tn=1024)
        else:
            o = i // 2
            proj = _norm_matmul(xt, mix_g, hgrn_w_in, i, o, tm=tm_big, tn=1024)
            heads = d // HGRN_HEAD_DIM
            mixed = _hgrn_scan(proj.reshape(bsz, seq, -1), lbs[i].reshape(1, -1), gains(hgrn_norm),
                               o, heads=heads, tt=ts, hb=8)
            xt = _out_proj(xt, [mixed.reshape(t, -1)], hgrn_w_out, o, tm=tm_big, tn=1024)
        xt = _ffn(xt, ffn2_g, ffn2_w_gate, ffn2_w_up, ffn2_w_down, i, tm=tm_big, tf=256)
        xt = _ple(xt, p3, ple_g, ple_w_gate, ple_w_proj, final_norm.reshape(1, -1).astype(F32), i,
                  tm=tm, final_norm=(i == depth - 1))
    return xt.reshape(bsz, seq, d)
```

```python
import functools
import math

import jax
import jax.numpy as jnp
from jax import lax
from jax.experimental import pallas as pl
from jax.experimental.pallas import tpu as pltpu

F32 = jnp.float32
BF16 = jnp.bfloat16

NORM_EPS = 1e-6
SUBLN_EPS = 1e-5
POOL_WINDOWS = (2, 4, 8, 16)
POOL_HALO = 16
DIFF_HEAD_DIM = 128
ROT_DIM = DIFF_HEAD_DIM // 4
ROPE_THETA = 500000.0
HGRN_HEAD_DIM = 128
HGRN_CHUNK = 16

LANES = 128
VMEM_LIMIT = 56 * 1024 * 1024


def _cparams(sem):
    return pltpu.CompilerParams(dimension_semantics=sem, vmem_limit_bytes=VMEM_LIMIT)


def _stacked(block, layer, index, **kwargs):
    return pl.BlockSpec((None,) + tuple(block), lambda *g: (layer,) + tuple(index(*g)), **kwargs)


def _rms(x, gain, eps):
    ms = jnp.mean(x * x, axis=-1, keepdims=True)
    return x * lax.rsqrt(ms + eps) * gain


def _dot(a, b):
    return jnp.dot(a, b, preferred_element_type=F32)


def _dot_nt(a, b):
    return lax.dot_general(a, b, (((1,), (1,)), ((), ())), preferred_element_type=F32)


def _dot_tn(a, b):
    return lax.dot_general(a, b, (((0,), (0,)), ((), ())), preferred_element_type=F32)


def _tree_sum(terms):
    while len(terms) > 1:
        terms = [terms[i] + terms[i + 1] for i in range(0, len(terms) - 1, 2)] + (
            [terms[-1]] if len(terms) % 2 else [])
    return terms[0]


def _ffn_kernel(x_ref, g_ref, wg_ref, wu_ref, wd_ref, o_ref, h_ref):
    j = pl.program_id(1)

    @pl.when(j == 0)
    def _():
        h_ref[...] = _rms(x_ref[...], g_ref[...], NORM_EPS).astype(BF16)
        o_ref[...] = jnp.zeros_like(o_ref)

    h = h_ref[...]
    a = _dot(h, wg_ref[...].astype(BF16))
    b = _dot(h, wu_ref[...].astype(BF16))
    act = (a * jax.nn.sigmoid(a) * b).astype(BF16)
    o_ref[...] += _dot(act, wd_ref[...].astype(BF16))

    @pl.when(j == pl.num_programs(1) - 1)
    def _():
        o_ref[...] = x_ref[...] + 0.5 * o_ref[...]


def _ffn(x, gain, w_gate, w_up, w_down, layer, *, tm, tf):
    t, d = x.shape
    f = w_gate.shape[2]
    return pl.pallas_call(
        _ffn_kernel,
        out_shape=jax.ShapeDtypeStruct((t, d), F32),
        grid=(t // tm, f // tf),
        in_specs=[
            pl.BlockSpec((tm, d), lambda i, j: (i, 0)),
            _stacked((1, d), layer, lambda i, j: (0, 0)),
            _stacked((d, tf), layer, lambda i, j: (0, j)),
            _stacked((d, tf), layer, lambda i, j: (0, j)),
            _stacked((tf, d), layer, lambda i, j: (j, 0)),
        ],
        out_specs=pl.BlockSpec((tm, d), lambda i, j: (i, 0)),
        scratch_shapes=[pltpu.VMEM((tm, d), BF16)],
        compiler_params=_cparams(("parallel", "arbitrary")),
        name="ffn",
    )(x, gain, w_gate, w_up, w_down)


def _norm_matmul_kernel(x_ref, g_ref, w_ref, o_ref, h_ref):
    @pl.when(pl.program_id(1) == 0)
    def _():
        h_ref[...] = _rms(x_ref[...], g_ref[...], NORM_EPS).astype(BF16)

    o_ref[...] = _dot(h_ref[...], w_ref[...].astype(BF16)).astype(o_ref.dtype)


def _norm_matmul(x, gain, w, layer, w_layer, *, tm, tn):
    t, d = x.shape
    n = w.shape[2]
    return pl.pallas_call(
        _norm_matmul_kernel,
        out_shape=jax.ShapeDtypeStruct((t, n), F32),
        grid=(t // tm, n // tn),
        in_specs=[
            pl.BlockSpec((tm, d), lambda i, j: (i, 0)),
            _stacked((1, d), layer, lambda i, j: (0, 0)),
            _stacked((d, tn), w_layer, lambda i, j: (0, j)),
        ],
        out_specs=pl.BlockSpec((tm, tn), lambda i, j: (i, j)),
        scratch_shapes=[pltpu.VMEM((tm, d), BF16)],
        compiler_params=_cparams(("parallel", "arbitrary")),
        name="norm_matmul",
    )(x, gain, w)


def _rope(y, cos, sin_signed):
    lane = lax.broadcasted_iota(jnp.int32, cos.shape, 1)
    half = ROT_DIM // 2
    outs = []
    for c in range(y.shape[1] // LANES):
        t = y[:, c * LANES:(c + 1) * LANES]
        swapped = jnp.where(lane < half, pltpu.roll(t, LANES - half, axis=1),
                            pltpu.roll(t, half, axis=1))
        outs.append(t * cos + swapped * sin_signed)
    return jnp.concatenate(outs, axis=1)


def _ab_proj_kernel(x_ref, g_ref, w_ref, pos_ref, freq_ref, sign_ref, u_ref, qkv_ref,
                    h_ref, cos_ref, sin_ref, *, q_scale, per):
    j = pl.program_id(1)
    sec = j // per

    @pl.when(j == 0)
    def _():
        h_ref[...] = _rms(x_ref[...], g_ref[...], NORM_EPS).astype(BF16)
        ang = pos_ref[...] * freq_ref[...]
        cos_ref[...] = jnp.cos(ang)
        sin_ref[...] = jnp.sin(ang) * sign_ref[...]

    y = _dot(h_ref[...], w_ref[...].astype(BF16))

    @pl.when(sec == 0)
    def _():
        u_ref[...] = y

    @pl.when(sec == 1)
    def _():
        qkv_ref[...] = (_rope(y, cos_ref[...], sin_ref[...]) * q_scale).astype(BF16)

    @pl.when(sec == 2)
    def _():
        qkv_ref[...] = _rope(y, cos_ref[...], sin_ref[...]).astype(BF16)

    @pl.when(sec == 3)
    def _():
        qkv_ref[...] = y.astype(BF16)


def _ab_proj(x, gain, w, pos, freq, sign, layer, w_layer, *, tm, tn):
    t, d = x.shape
    n = w.shape[2]
    per = n // 4 // tn
    q_scale = DIFF_HEAD_DIM ** -0.5 * math.log2(math.e)
    return pl.pallas_call(
        functools.partial(_ab_proj_kernel, q_scale=q_scale, per=per),
        out_shape=(jax.ShapeDtypeStruct((t, n // 4), F32),
                   jax.ShapeDtypeStruct((t, 3 * n // 4), BF16)),
        grid=(t // tm, 4 * per),
        in_specs=[
            pl.BlockSpec((tm, d), lambda i, j: (i, 0)),
            _stacked((1, d), layer, lambda i, j: (0, 0)),
            _stacked((d, tn), w_layer, lambda i, j: (0, j)),
            pl.BlockSpec((tm, 1), lambda i, j: (i, 0)),
            pl.BlockSpec((1, LANES), lambda i, j: (0, 0)),
            pl.BlockSpec((1, LANES), lambda i, j: (0, 0)),
        ],
        out_specs=(
            pl.BlockSpec((tm, tn), lambda i, j: (i, jnp.minimum(j, per - 1))),
            pl.BlockSpec((tm, tn), lambda i, j: (i, jnp.maximum(j - per, 0))),
        ),
        scratch_shapes=[pltpu.VMEM((tm, d), BF16), pltpu.VMEM((tm, LANES), F32),
                        pltpu.VMEM((tm, LANES), F32)],
        compiler_params=_cparams(("parallel", "arbitrary")),
        name="ab_proj",
    )(x, gain, w, pos, freq, sign)


def _pool_kernel(u_ref, w_ref, s_ref, o_ref, ext_ref, *, ts):
    i = pl.program_id(1)

    @pl.when(i == 0)
    def _():
        ext_ref[0:POOL_HALO, :] = jnp.zeros((POOL_HALO, ext_ref.shape[1]), F32)

    ext_ref[POOL_HALO:, :] = u_ref[0]
    gd = w_ref.shape[1]
    row = lax.broadcasted_iota(jnp.int32, (ts, gd), 0) + i * ts
    for g, win in enumerate(POOL_WINDOWS):
        cols = slice(g * gd, (g + 1) * gd)
        acc = ext_ref[:, cols]
        shift = 1
        while shift < win:
            acc = acc + pltpu.roll(acc, shift, axis=0)
            shift *= 2
        cnt = jnp.minimum(row + 1, win).astype(F32)
        v = u_ref[0, :, cols]
        dlt = acc[POOL_HALO:, :] / cnt - v
        y = _dot(dlt.astype(BF16), w_ref[g].astype(BF16)) * s_ref[:, cols]
        o_ref[0, :, cols] = y.astype(o_ref.dtype)
    ext_ref[0:POOL_HALO, :] = ext_ref[ts:ts + POOL_HALO, :]


def _pool(u3, pool_w, pool_scale, layer, *, ts):
    b, s, pw = u3.shape
    _, g, gd, _ = pool_w.shape
    return pl.pallas_call(
        functools.partial(_pool_kernel, ts=ts),
        out_shape=jax.ShapeDtypeStruct((b, s, pw), BF16),
        grid=(b, s // ts),
        in_specs=[
            pl.BlockSpec((1, ts, pw), lambda bi, i: (bi, i, 0)),
            _stacked((g, gd, gd), layer, lambda bi, i: (0, 0, 0)),
            _stacked((1, pw), layer, lambda bi, i: (0, 0)),
        ],
        out_specs=pl.BlockSpec((1, ts, pw), lambda bi, i: (bi, i, 0)),
        scratch_shapes=[pltpu.VMEM((ts + POOL_HALO, pw), F32)],
        compiler_params=_cparams(("parallel", "arbitrary")),
        name="pool",
    )(u3, pool_w, pool_scale)


def _attn_kernel(lam_ref, q_ref, k_ref, v_ref, sg_ref, o_ref, m_ref, l_ref, acc_ref, s_ref,
                 p_ref, a_ref, *, tq, rs, lam_init):
    dh = DIFF_HEAD_DIM
    strip = 16
    qi = pl.program_id(2)
    m_ref[...] = jnp.full(m_ref.shape, -jnp.inf, F32)
    l_ref[...] = jnp.zeros(l_ref.shape, F32)
    acc_ref[...] = jnp.zeros(acc_ref.shape, F32)

    def step(kb, masked):
        r0 = pl.multiple_of(kb * tq, tq)

        def width(r):
            return (r + 1) * rs if masked else tq

        def qk(r):
            nk = width(r)
            for c in range(2):
                comp = slice(c * dh, (c + 1) * dh)
                s_ref[r % 2, c, :, :nk] = _dot_nt(q_ref[0, r * rs:(r + 1) * rs, comp],
                                                  k_ref[0, pl.ds(r0, nk), comp])

        def softmax(r):
            nk = width(r)
            for c in range(2):
                for i in range(rs // strip):
                    lo = r * rs + i * strip
                    rows = slice(lo, lo + strip)
                    ns = min(nk, -(-(lo + strip) // LANES) * LANES) if masked else nk
                    s = s_ref[r % 2, c, i * strip:(i + 1) * strip, :ns]
                    if masked:
                        row = lax.broadcasted_iota(jnp.int32, s.shape, 0) + lo
                        col = lax.broadcasted_iota(jnp.int32, s.shape, 1)
                        s = jnp.where(col <= row, s, -jnp.inf)
                    m_prev = m_ref[c, rows]
                    m_new = jnp.maximum(m_prev, jnp.max(s, axis=-1, keepdims=True))
                    alpha = jnp.exp2(m_prev - m_new)
                    p = jnp.exp2(s - m_new)
                    part = _tree_sum([p[:, j * LANES:(j + 1) * LANES] for j in range(ns // LANES)])
                    l_ref[c, rows] = alpha * l_ref[c, rows] + part
                    m_ref[c, rows] = m_new
                    a_ref[r % 2, c, i * strip:(i + 1) * strip] = alpha
                    prow = slice(c * rs + i * strip, c * rs + (i + 1) * strip)
                    p_ref[r % 2, prow, :ns] = p.astype(BF16)
                    if ns < nk:
                        p_ref[r % 2, prow, ns:nk] = jnp.zeros((strip, nk - ns), BF16)

        def pv(r):
            nk = width(r)
            out = _dot(p_ref[r % 2, :, :nk], v_ref[0, pl.ds(r0, nk), :])
            rows = slice(r * rs, (r + 1) * rs)
            acc_ref[0, rows] = a_ref[r % 2, 0] * acc_ref[0, rows] + out[:rs]
            acc_ref[1, rows] = a_ref[r % 2, 1] * acc_ref[1, rows] + out[rs:]

        nr = tq // rs
        qk(0)
        for r in range(nr):
            if r + 1 < nr:
                qk(r + 1)
            softmax(r)
            pv(r)

    def body(kb, carry):
        step(kb, False)
        return carry

    lax.fori_loop(0, qi, body, 0)
    step(qi, True)

    lp = lam_ref[...]
    lam = (jnp.exp(jnp.sum(lp[0:1] * lp[1:2], axis=-1, keepdims=True))
           - jnp.exp(jnp.sum(lp[2:3] * lp[3:4], axis=-1, keepdims=True)) + lam_init)
    l0 = jnp.sum(l_ref[0], axis=-1, keepdims=True)
    l1 = jnp.sum(l_ref[1], axis=-1, keepdims=True)
    o = acc_ref[0] / l0 - lam * (acc_ref[1] / l1)
    o = _rms(o, sg_ref[...], SUBLN_EPS) * (1.0 - lam_init)
    o_ref[0] = o.astype(o_ref.dtype)


def _diff_attention(qkv3, lam_params, subln, lam_init, layer, *, heads, tq, rs):
    b, s, w3 = qkv3.shape
    hw = 2 * DIFF_HEAD_DIM
    return pl.pallas_call(
        functools.partial(_attn_kernel, tq=tq, rs=rs, lam_init=lam_init),
        out_shape=jax.ShapeDtypeStruct((b, s, w3 // 3), BF16),
        grid=(b, heads, s // tq),
        in_specs=[
            _stacked((4, DIFF_HEAD_DIM), layer, lambda bi, h, i: (0, 0)),
            pl.BlockSpec((1, tq, hw), lambda bi, h, i: (bi, i, h)),
            pl.BlockSpec((1, s, hw), lambda bi, h, i: (bi, 0, heads + h),
                         pipeline_mode=pl.Buffered(1)),
            pl.BlockSpec((1, s, hw), lambda bi, h, i: (bi, 0, 2 * heads + h),
                         pipeline_mode=pl.Buffered(1)),
            _stacked((1, hw), layer, lambda bi, h, i: (0, 0)),
        ],
        out_specs=pl.BlockSpec((1, tq, hw), lambda bi, h, i: (bi, i, h)),
        scratch_shapes=[pltpu.VMEM((2, tq, 1), F32), pltpu.VMEM((2, tq, LANES), F32),
                        pltpu.VMEM((2, tq, hw), F32), pltpu.VMEM((2, 2, rs, tq), F32),
                        pltpu.VMEM((2, 2 * rs, tq), BF16), pltpu.VMEM((2, 2, rs, 1), F32)],
        compiler_params=_cparams(("parallel", "parallel", "arbitrary")),
        name="diff_attention",
    )(lam_params, qkv3, qkv3, qkv3, subln)


def _out_proj_kernel(*refs):
    x_ref, w_ref, o_ref, wb_ref = refs[0], refs[-3], refs[-2], refs[-1]

    @pl.when(pl.program_id(0) == 0)
    def _():
        wb_ref[...] = w_ref[...].astype(BF16)

    acc = x_ref[...]
    k0 = 0
    for a_ref in refs[1:-3]:
        kw = a_ref.shape[1]
        acc = acc + _dot(a_ref[...], wb_ref[k0:k0 + kw, :])
        k0 += kw
    o_ref[...] = acc


def _out_proj(x, parts, w, layer, *, tm):
    t, d = x.shape
    k = w.shape[1]
    in_specs = [pl.BlockSpec((tm, d), lambda i: (i, 0))]
    in_specs += [pl.BlockSpec((tm, a.shape[1]), lambda i: (i, 0)) for a in parts]
    in_specs += [_stacked((k, d), layer, lambda i: (0, 0), pipeline_mode=pl.Buffered(1))]
    return pl.pallas_call(
        _out_proj_kernel,
        out_shape=jax.ShapeDtypeStruct((t, d), F32),
        grid=(t // tm,),
        in_specs=in_specs,
        out_specs=pl.BlockSpec((tm, d), lambda i: (i, 0)),
        scratch_shapes=[pltpu.VMEM((k, d), BF16)],
        compiler_params=_cparams(("arbitrary",)),
        name="out_proj",
    )(x, *parts, w)


def _ple_kernel(x_ref, p_ref, g_ref, wg_ref, wp_ref, fg_ref, o_ref, wgb_ref, wpb_ref, *,
                final_norm):
    @pl.when(pl.program_id(0) == 0)
    def _():
        wgb_ref[...] = wg_ref[...].astype(BF16)
        wpb_ref[...] = wp_ref[...].astype(BF16)

    x = x_ref[...]
    h = _rms(x, g_ref[...], NORM_EPS).astype(BF16)
    gate = jax.nn.sigmoid(_dot(h, wgb_ref[...]))
    y = x + _dot(p_ref[...].astype(BF16), wpb_ref[...]) * gate
    if final_norm:
        y = _rms(y, fg_ref[...], NORM_EPS)
    o_ref[...] = y


def _ple(x, p, gain, w_gate, w_proj, final_gain, layer, *, tm, final_norm):
    t, d = x.shape
    pd = p.shape[2]
    once = pl.Buffered(1)
    return pl.pallas_call(
        functools.partial(_ple_kernel, final_norm=final_norm),
        out_shape=jax.ShapeDtypeStruct((t, d), F32),
        grid=(t // tm,),
        in_specs=[
            pl.BlockSpec((tm, d), lambda i: (i, 0)),
            _stacked((tm, pd), layer, lambda i: (i, 0)),
            _stacked((1, d), layer, lambda i: (0, 0)),
            _stacked((d, d), layer, lambda i: (0, 0), pipeline_mode=once),
            _stacked((pd, d), layer, lambda i: (0, 0), pipeline_mode=once),
            pl.BlockSpec((1, d), lambda i: (0, 0)),
        ],
        out_specs=pl.BlockSpec((tm, d), lambda i: (i, 0)),
        scratch_shapes=[pltpu.VMEM((d, d), BF16), pltpu.VMEM((pd, d), BF16)],
        compiler_params=_cparams(("arbitrary",)),
        name="ple",
    )(x, p, gain, w_gate, w_proj, final_gain)


def _hgrn_kernel(q_ref, f_ref, v_ref, g_ref, lb_ref, ng_ref, o_ref, st_ref, qb_ref, gb_ref,
                 hb_ref, ob_ref, *, tt, hb):
    c16 = HGRN_CHUNK
    half = c16 // 2
    hd = HGRN_HEAD_DIM
    width = hb * hd
    log2e = math.log2(math.e)

    @pl.when(pl.program_id(2) == 0)
    def _():
        st_ref[...] = jnp.zeros(st_ref.shape, F32)

    q = q_ref[0]
    qb_ref[...] = q * jax.nn.sigmoid(q)
    ff = f_ref[0]
    lb = lb_ref[...]
    log_lb = jnp.log(lb)
    log_sig = jnp.minimum(ff, 0.0) - jnp.log(1.0 + jnp.exp(-jnp.abs(ff)))
    b = jnp.log1p(-lb) + log_sig
    log_f = jnp.maximum(log_lb, b) + jnp.log(1.0 + jnp.exp(-jnp.abs(log_lb - b)))
    log_k = b - ff
    row = lax.broadcasted_iota(jnp.int32, (tt, width), 0) & (c16 - 1)
    cum = log_f
    shift = 1
    while shift < c16:
        cum = cum + jnp.where(row >= shift, pltpu.roll(cum, shift, axis=0), 0.0)
        shift *= 2
    gb_ref[...] = cum * log2e
    hb_ref[...] = (cum - log_k) * log2e

    ones = jnp.ones((hd, hd), BF16)
    t8 = lax.broadcasted_iota(jnp.int32, (half, hd), 0)

    def chunk(c, carry):
        r0 = pl.multiple_of(c * c16, c16)
        for h in range(hb):
            cols = slice(h * hd, (h + 1) * hd)
            g2 = gb_ref[pl.ds(r0, c16), cols]
            h2 = hb_ref[pl.ds(r0, c16), cols]
            qc = qb_ref[pl.ds(r0, c16), cols]
            vc = v_ref[0, pl.ds(r0, c16), cols]
            g_top, g_bot = g2[:half], g2[half:]
            q_top, q_bot = qc[:half], qc[half:]
            lhs = []
            for s in range(half):
                hs = h2[s:s + 1, :]
                d_top = g_top - hs
                if s > 0:
                    d_top = jnp.where(t8 >= s, d_top, -jnp.inf)
                p = jnp.concatenate([q_top * jnp.exp2(d_top), q_bot * jnp.exp2(g_bot - hs)], axis=0)
                lhs.append(p.astype(BF16))
            bots = []
            for s in range(half, c16):
                d_bot = g_bot - h2[s:s + 1, :]
                if s > half:
                    d_bot = jnp.where(t8 >= s - half, d_bot, -jnp.inf)
                bots.append(q_bot * jnp.exp2(d_bot))
            for j in range(0, half, 2):
                lhs.append(jnp.concatenate([bots[j], bots[j + 1]], axis=0).astype(BF16))
            red = _dot(jnp.concatenate(lhs, axis=0), ones)
            top_terms = [red[s * c16:s * c16 + half] * vc[s:s + 1, :] for s in range(half)]
            bot_terms = [red[s * c16 + half:(s + 1) * c16] * vc[s:s + 1, :] for s in range(half)]
            base = half * c16
            bot_terms += [red[base + j * half:base + (j + 1) * half] * vc[half + j:half + j + 1, :]
                          for j in range(half)]
            intra = jnp.concatenate([_tree_sum(top_terms), _tree_sum(bot_terms)], axis=0)
            state = st_ref[h]
            inter = _dot_nt((qc * jnp.exp2(g2)).astype(BF16), state.astype(BF16))
            ob_ref[pl.ds(r0, c16), cols] = inter + intra
            g_last = g2[c16 - 1:c16, :]
            kdec = jnp.exp2(g_last - h2).astype(BF16)
            st_ref[h] = state * jnp.exp2(g_last) + _dot_tn(vc.astype(BF16), kdec)
        return carry

    lax.fori_loop(0, tt // c16, chunk, 0)

    for h in range(hb):
        cols = slice(h * hd, (h + 1) * hd)
        gate = g_ref[0, :, cols]
        y = _rms(ob_ref[:, cols], ng_ref[...], NORM_EPS) * (gate * jax.nn.sigmoid(gate))
        o_ref[0, :, cols] = y.astype(o_ref.dtype)


def _hgrn_scan(proj3, lb, norm_gain, layer, *, heads, tt, hb):
    b, s, w4 = proj3.shape
    hd = HGRN_HEAD_DIM
    width = hb * hd
    ng = heads // hb

    def sec(k):
        return pl.BlockSpec((1, tt, width), lambda bi, hg, i: (bi, i, k * ng + hg))

    return pl.pallas_call(
        functools.partial(_hgrn_kernel, tt=tt, hb=hb),
        out_shape=jax.ShapeDtypeStruct((b, s, w4 // 4), BF16),
        grid=(b, ng, s // tt),
        in_specs=[sec(0), sec(1), sec(2), sec(3),
                  pl.BlockSpec((1, width), lambda bi, hg, i: (0, hg)),
                  _stacked((1, hd), layer, lambda bi, hg, i: (0, 0))],
        out_specs=pl.BlockSpec((1, tt, width), lambda bi, hg, i: (bi, i, hg)),
        scratch_shapes=[pltpu.VMEM((hb, hd, hd), F32), pltpu.VMEM((tt, width), F32),
                        pltpu.VMEM((tt, width), F32), pltpu.VMEM((tt, width), F32),
                        pltpu.VMEM((tt, width), F32)],
        compiler_params=_cparams(("parallel", "parallel", "arbitrary")),
        name="hgrn_scan",
    )(proj3, proj3, proj3, proj3, lb, norm_gain)


def kernel(x, p, positions, ffn1_norm, ffn1_w_gate, ffn1_w_up, ffn1_w_down, mix_norm, ffn2_norm, ffn2_w_gate, ffn2_w_up, ffn2_w_down, ple_norm, ple_w_gate, ple_w_proj, ab_w_in, pool_w, pool_scale, diff_lambda, diff_subln, ab_w_out, hgrn_w_in, hgrn_lower_bounds, hgrn_norm, hgrn_w_out, final_norm):
    bsz, seq, d = x.shape
    depth = p.shape[0]
    t = bsz * seq
    tm_big = min(1024, t)
    tm = min(512, t)
    ts = min(512, seq)

    def gains(a):
        return a.reshape(a.shape[0], 1, a.shape[-1]).astype(F32)

    half = ROT_DIM // 2
    inv_freq = ROPE_THETA ** (-jnp.arange(0, ROT_DIM, 2, dtype=F32) / ROT_DIM)
    zeros = jnp.zeros((LANES - ROT_DIM,), F32)
    freq = jnp.concatenate([inv_freq, inv_freq, zeros]).reshape(1, LANES)
    sign = jnp.concatenate([-jnp.ones((half,), F32), jnp.ones((half,), F32), zeros]).reshape(1, LANES)
    pos = positions.reshape(t, 1).astype(F32)

    lbs = jax.nn.softmax(hgrn_lower_bounds.astype(F32), axis=0)
    lbs = jnp.cumsum(lbs, axis=0) - lbs[0]

    ffn1_g, ffn2_g, mix_g, ple_g = gains(ffn1_norm), gains(ffn2_norm), gains(mix_norm), gains(ple_norm)
    p3 = p.reshape(depth, t, p.shape[-1])
    xt = x.reshape(t, d)
    for i in range(depth):
        xt = _ffn(xt, ffn1_g, ffn1_w_gate, ffn1_w_up, ffn1_w_down, i, tm=tm_big, tf=256)
        if i % 2 == 0:
            e = i // 2
            lam_init = 0.8 - 0.6 * math.exp(-0.3 * i)
            u, qkv = _ab_proj(xt, mix_g, ab_w_in, pos, freq, sign, i, e, tm=tm_big, tn=512)
            pw = u.shape[1]
            a_out = _pool(u.reshape(bsz, seq, pw), pool_w, gains(pool_scale), e, ts=ts)
            heads = qkv.shape[1] // (3 * 2 * DIFF_HEAD_DIM)
            b_out = _diff_attention(qkv.reshape(bsz, seq, -1), diff_lambda.astype(F32),
                                    gains(diff_subln), lam_init, e, heads=heads,
                                    tq=min(2048, seq), rs=min(512, seq))
            xt = _out_proj(xt, [a_out.reshape(t, -1), b_out.reshape(t, -1)], ab_w_out, e, tm=tm)
        else:
            o = i // 2
            proj = _norm_matmul(xt, mix_g, hgrn_w_in, i, o, tm=tm_big, tn=1024)
            heads = d // HGRN_HEAD_DIM
            mixed = _hgrn_scan(proj.reshape(bsz, seq, -1), lbs[i].reshape(1, -1), gains(hgrn_norm),
                               o, heads=heads, tt=min(256, seq), hb=16)
            xt = _out_proj(xt, [mixed.reshape(t, -1)], hgrn_w_out, o, tm=tm)
        xt = _ffn(xt, ffn2_g, ffn2_w_gate, ffn2_w_up, ffn2_w_down, i, tm=tm_big, tf=256)
        xt = _ple(xt, p3, ple_g, ple_w_gate, ple_w_proj, final_norm.reshape(1, -1).astype(F32), i,
                  tm=tm, final_norm=(i == depth - 1))
    return xt.reshape(bsz, seq, d)
```

```python
import functools
import math

import jax
import jax.numpy as jnp
from jax import lax
from jax.experimental import pallas as pl
from jax.experimental.pallas import tpu as pltpu

F32 = jnp.float32
BF16 = jnp.bfloat16

NORM_EPS = 1e-6
SUBLN_EPS = 1e-5
POOL_WINDOWS = (2, 4, 8, 16)
POOL_HALO = 16
DIFF_HEAD_DIM = 128
ROT_DIM = DIFF_HEAD_DIM // 4
ROPE_THETA = 500000.0
HGRN_HEAD_DIM = 128
HGRN_CHUNK = 16
HGRN_SAFE_BITS = 80.0

LANES = 128
VMEM_LIMIT = 56 * 1024 * 1024


def _cparams(sem):
    return pltpu.CompilerParams(dimension_semantics=sem, vmem_limit_bytes=VMEM_LIMIT)


def _stacked(block, layer, index, **kwargs):
    return pl.BlockSpec((None,) + tuple(block), lambda *g: (layer,) + tuple(index(*g)), **kwargs)


def _rms(x, gain, eps):
    ms = jnp.mean(x * x, axis=-1, keepdims=True)
    return x * lax.rsqrt(ms + eps) * gain


def _dot(a, b):
    return jnp.dot(a, b, preferred_element_type=F32)


def _dot_nt(a, b):
    return lax.dot_general(a, b, (((1,), (1,)), ((), ())), preferred_element_type=F32)


def _dot_tn(a, b):
    return lax.dot_general(a, b, (((0,), (0,)), ((), ())), preferred_element_type=F32)


def _tree_sum(terms):
    while len(terms) > 1:
        terms = [terms[i] + terms[i + 1] for i in range(0, len(terms) - 1, 2)] + (
            [terms[-1]] if len(terms) % 2 else [])
    return terms[0]


def _ffn_kernel(x_ref, g_ref, wg_ref, wu_ref, wd_ref, o_ref, h_ref):
    j = pl.program_id(1)

    @pl.when(j == 0)
    def _():
        h_ref[...] = _rms(x_ref[...], g_ref[...], NORM_EPS).astype(BF16)
        o_ref[...] = jnp.zeros_like(o_ref)

    h = h_ref[...]
    a = _dot(h, wg_ref[...].astype(BF16))
    b = _dot(h, wu_ref[...].astype(BF16))
    act = (a * jax.nn.sigmoid(a) * b).astype(BF16)
    o_ref[...] += _dot(act, wd_ref[...].astype(BF16))

    @pl.when(j == pl.num_programs(1) - 1)
    def _():
        o_ref[...] = x_ref[...] + 0.5 * o_ref[...]


def _ffn(x, gain, w_gate, w_up, w_down, layer, *, tm, tf):
    t, d = x.shape
    f = w_gate.shape[2]
    return pl.pallas_call(
        _ffn_kernel,
        out_shape=jax.ShapeDtypeStruct((t, d), F32),
        grid=(t // tm, f // tf),
        in_specs=[
            pl.BlockSpec((tm, d), lambda i, j: (i, 0)),
            _stacked((1, d), layer, lambda i, j: (0, 0)),
            _stacked((d, tf), layer, lambda i, j: (0, j)),
            _stacked((d, tf), layer, lambda i, j: (0, j)),
            _stacked((tf, d), layer, lambda i, j: (j, 0)),
        ],
        out_specs=pl.BlockSpec((tm, d), lambda i, j: (i, 0)),
        scratch_shapes=[pltpu.VMEM((tm, d), BF16)],
        compiler_params=_cparams(("parallel", "arbitrary")),
        name="ffn",
    )(x, gain, w_gate, w_up, w_down)


def _norm_matmul_kernel(x_ref, g_ref, w_ref, o_ref, h_ref):
    @pl.when(pl.program_id(1) == 0)
    def _():
        h_ref[...] = _rms(x_ref[...], g_ref[...], NORM_EPS).astype(BF16)

    o_ref[...] = _dot(h_ref[...], w_ref[...].astype(BF16)).astype(o_ref.dtype)


def _norm_matmul(x, gain, w, layer, w_layer, *, tm, tn):
    t, d = x.shape
    n = w.shape[2]
    return pl.pallas_call(
        _norm_matmul_kernel,
        out_shape=jax.ShapeDtypeStruct((t, n), F32),
        grid=(t // tm, n // tn),
        in_specs=[
            pl.BlockSpec((tm, d), lambda i, j: (i, 0)),
            _stacked((1, d), layer, lambda i, j: (0, 0)),
            _stacked((d, tn), w_layer, lambda i, j: (0, j)),
        ],
        out_specs=pl.BlockSpec((tm, tn), lambda i, j: (i, j)),
        scratch_shapes=[pltpu.VMEM((tm, d), BF16)],
        compiler_params=_cparams(("parallel", "arbitrary")),
        name="norm_matmul",
    )(x, gain, w)


def _rope(y, cos, sin_signed):
    lane = lax.broadcasted_iota(jnp.int32, cos.shape, 1)
    half = ROT_DIM // 2
    outs = []
    for c in range(y.shape[1] // LANES):
        t = y[:, c * LANES:(c + 1) * LANES]
        swapped = jnp.where(lane < half, pltpu.roll(t, LANES - half, axis=1),
                            pltpu.roll(t, half, axis=1))
        outs.append(t * cos + swapped * sin_signed)
    return jnp.concatenate(outs, axis=1)


def _ab_proj_kernel(x_ref, g_ref, w_ref, pos_ref, freq_ref, sign_ref, u_ref, qkv_ref,
                    h_ref, cos_ref, sin_ref, *, q_scale, per):
    j = pl.program_id(1)
    sec = j // per

    @pl.when(j == 0)
    def _():
        h_ref[...] = _rms(x_ref[...], g_ref[...], NORM_EPS).astype(BF16)
        ang = pos_ref[...] * freq_ref[...]
        cos_ref[...] = jnp.cos(ang)
        sin_ref[...] = jnp.sin(ang) * sign_ref[...]

    y = _dot(h_ref[...], w_ref[...].astype(BF16))

    @pl.when(sec == 0)
    def _():
        u_ref[...] = y

    @pl.when(sec == 1)
    def _():
        qkv_ref[...] = (_rope(y, cos_ref[...], sin_ref[...]) * q_scale).astype(BF16)

    @pl.when(sec == 2)
    def _():
        qkv_ref[...] = _rope(y, cos_ref[...], sin_ref[...]).astype(BF16)

    @pl.when(sec == 3)
    def _():
        qkv_ref[...] = y.astype(BF16)


def _ab_proj(x, gain, w, pos, freq, sign, layer, w_layer, *, tm, tn):
    t, d = x.shape
    n = w.shape[2]
    per = n // 4 // tn
    q_scale = DIFF_HEAD_DIM ** -0.5 * math.log2(math.e)
    return pl.pallas_call(
        functools.partial(_ab_proj_kernel, q_scale=q_scale, per=per),
        out_shape=(jax.ShapeDtypeStruct((t, n // 4), F32),
                   jax.ShapeDtypeStruct((t, 3 * n // 4), BF16)),
        grid=(t // tm, 4 * per),
        in_specs=[
            pl.BlockSpec((tm, d), lambda i, j: (i, 0)),
            _stacked((1, d), layer, lambda i, j: (0, 0)),
            _stacked((d, tn), w_layer, lambda i, j: (0, j)),
            pl.BlockSpec((tm, 1), lambda i, j: (i, 0)),
            pl.BlockSpec((1, LANES), lambda i, j: (0, 0)),
            pl.BlockSpec((1, LANES), lambda i, j: (0, 0)),
        ],
        out_specs=(
            pl.BlockSpec((tm, tn), lambda i, j: (i, jnp.minimum(j, per - 1))),
            pl.BlockSpec((tm, tn), lambda i, j: (i, jnp.maximum(j - per, 0))),
        ),
        scratch_shapes=[pltpu.VMEM((tm, d), BF16), pltpu.VMEM((tm, LANES), F32),
                        pltpu.VMEM((tm, LANES), F32)],
        compiler_params=_cparams(("parallel", "arbitrary")),
        name="ab_proj",
    )(x, gain, w, pos, freq, sign)


def _pool_kernel(u_ref, w_ref, s_ref, o_ref, ext_ref, *, ts):
    i = pl.program_id(1)

    @pl.when(i == 0)
    def _():
        ext_ref[0:POOL_HALO, :] = jnp.zeros((POOL_HALO, ext_ref.shape[1]), F32)

    ext_ref[POOL_HALO:, :] = u_ref[0]
    gd = w_ref.shape[1]
    row = lax.broadcasted_iota(jnp.int32, (ts, gd), 0) + i * ts
    for g, win in enumerate(POOL_WINDOWS):
        cols = slice(g * gd, (g + 1) * gd)
        acc = ext_ref[:, cols]
        shift = 1
        while shift < win:
            acc = acc + pltpu.roll(acc, shift, axis=0)
            shift *= 2
        cnt = jnp.minimum(row + 1, win).astype(F32)
        v = u_ref[0, :, cols]
        dlt = acc[POOL_HALO:, :] / cnt - v
        y = _dot(dlt.astype(BF16), w_ref[g].astype(BF16)) * s_ref[:, cols]
        o_ref[0, :, cols] = y.astype(o_ref.dtype)
    ext_ref[0:POOL_HALO, :] = ext_ref[ts:ts + POOL_HALO, :]


def _pool(u3, pool_w, pool_scale, layer, *, ts):
    b, s, pw = u3.shape
    _, g, gd, _ = pool_w.shape
    return pl.pallas_call(
        functools.partial(_pool_kernel, ts=ts),
        out_shape=jax.ShapeDtypeStruct((b, s, pw), BF16),
        grid=(b, s // ts),
        in_specs=[
            pl.BlockSpec((1, ts, pw), lambda bi, i: (bi, i, 0)),
            _stacked((g, gd, gd), layer, lambda bi, i: (0, 0, 0)),
            _stacked((1, pw), layer, lambda bi, i: (0, 0)),
        ],
        out_specs=pl.BlockSpec((1, ts, pw), lambda bi, i: (bi, i, 0)),
        scratch_shapes=[pltpu.VMEM((ts + POOL_HALO, pw), F32)],
        compiler_params=_cparams(("parallel", "arbitrary")),
        name="pool",
    )(u3, pool_w, pool_scale)


def _attn_kernel(lam_ref, q_ref, k_ref, v_ref, sg_ref, o_ref, m_ref, l_ref, acc_ref, s_ref,
                 p_ref, a_ref, *, tq, rs, lam_init):
    dh = DIFF_HEAD_DIM
    strip = 16
    qi = pl.program_id(2)
    m_ref[...] = jnp.full(m_ref.shape, -jnp.inf, F32)
    l_ref[...] = jnp.zeros(l_ref.shape, F32)
    acc_ref[...] = jnp.zeros(acc_ref.shape, F32)

    def step(kb, masked):
        r0 = pl.multiple_of(kb * tq, tq)

        def width(r):
            return (r + 1) * rs if masked else tq

        def qk(r):
            nk = width(r)
            for c in range(2):
                comp = slice(c * dh, (c + 1) * dh)
                s_ref[r % 2, c, :, :nk] = _dot_nt(q_ref[0, r * rs:(r + 1) * rs, comp],
                                                  k_ref[0, pl.ds(r0, nk), comp])

        def softmax(r):
            nk = width(r)
            for c in range(2):
                for i in range(rs // strip):
                    lo = r * rs + i * strip
                    rows = slice(lo, lo + strip)
                    ns = min(nk, -(-(lo + strip) // LANES) * LANES) if masked else nk
                    s = s_ref[r % 2, c, i * strip:(i + 1) * strip, :ns]
                    if masked:
                        row = lax.broadcasted_iota(jnp.int32, s.shape, 0) + lo
                        col = lax.broadcasted_iota(jnp.int32, s.shape, 1)
                        s = jnp.where(col <= row, s, -jnp.inf)
                    m_prev = m_ref[c, rows]
                    m_new = jnp.maximum(m_prev, jnp.max(s, axis=-1, keepdims=True))
                    alpha = jnp.exp2(m_prev - m_new)
                    p = jnp.exp2(s - m_new)
                    part = _tree_sum([p[:, j * LANES:(j + 1) * LANES] for j in range(ns // LANES)])
                    l_ref[c, rows] = alpha * l_ref[c, rows] + part
                    m_ref[c, rows] = m_new
                    a_ref[r % 2, c, i * strip:(i + 1) * strip] = alpha
                    prow = slice(c * rs + i * strip, c * rs + (i + 1) * strip)
                    p_ref[r % 2, prow, :ns] = p.astype(BF16)
                    if ns < nk:
                        p_ref[r % 2, prow, ns:nk] = jnp.zeros((strip, nk - ns), BF16)

        def pv(r):
            nk = width(r)
            out = _dot(p_ref[r % 2, :, :nk], v_ref[0, pl.ds(r0, nk), :])
            rows = slice(r * rs, (r + 1) * rs)
            acc_ref[0, rows] = a_ref[r % 2, 0] * acc_ref[0, rows] + out[:rs]
            acc_ref[1, rows] = a_ref[r % 2, 1] * acc_ref[1, rows] + out[rs:]

        nr = tq // rs
        qk(0)
        for r in range(nr):
            if r + 1 < nr:
                qk(r + 1)
            softmax(r)
            pv(r)

    def body(kb, carry):
        step(kb, False)
        return carry

    lax.fori_loop(0, qi, body, 0)
    step(qi, True)

    lp = lam_ref[...]
    lam = (jnp.exp(jnp.sum(lp[0:1] * lp[1:2], axis=-1, keepdims=True))
           - jnp.exp(jnp.sum(lp[2:3] * lp[3:4], axis=-1, keepdims=True)) + lam_init)
    l0 = jnp.sum(l_ref[0], axis=-1, keepdims=True)
    l1 = jnp.sum(l_ref[1], axis=-1, keepdims=True)
    o = acc_ref[0] / l0 - lam * (acc_ref[1] / l1)
    o = _rms(o, sg_ref[...], SUBLN_EPS) * (1.0 - lam_init)
    o_ref[0] = o.astype(o_ref.dtype)


def _diff_attention(qkv3, lam_params, subln, lam_init, layer, *, heads, tq, rs):
    b, s, w3 = qkv3.shape
    hw = 2 * DIFF_HEAD_DIM
    return pl.pallas_call(
        functools.partial(_attn_kernel, tq=tq, rs=rs, lam_init=lam_init),
        out_shape=jax.ShapeDtypeStruct((b, s, w3 // 3), BF16),
        grid=(b, heads, s // tq),
        in_specs=[
            _stacked((4, DIFF_HEAD_DIM), layer, lambda bi, h, i: (0, 0)),
            pl.BlockSpec((1, tq, hw), lambda bi, h, i: (bi, i, h)),
            pl.BlockSpec((1, s, hw), lambda bi, h, i: (bi, 0, heads + h),
                         pipeline_mode=pl.Buffered(1)),
            pl.BlockSpec((1, s, hw), lambda bi, h, i: (bi, 0, 2 * heads + h),
                         pipeline_mode=pl.Buffered(1)),
            _stacked((1, hw), layer, lambda bi, h, i: (0, 0)),
        ],
        out_specs=pl.BlockSpec((1, tq, hw), lambda bi, h, i: (bi, i, h)),
        scratch_shapes=[pltpu.VMEM((2, tq, 1), F32), pltpu.VMEM((2, tq, LANES), F32),
                        pltpu.VMEM((2, tq, hw), F32), pltpu.VMEM((2, 2, rs, tq), F32),
                        pltpu.VMEM((2, 2 * rs, tq), BF16), pltpu.VMEM((2, 2, rs, 1), F32)],
        compiler_params=_cparams(("parallel", "parallel", "arbitrary")),
        name="diff_attention",
    )(lam_params, qkv3, qkv3, qkv3, subln)


def _out_proj_kernel(*refs):
    x_ref, w_ref, o_ref, wb_ref = refs[0], refs[-3], refs[-2], refs[-1]

    @pl.when(pl.program_id(0) == 0)
    def _():
        wb_ref[...] = w_ref[...].astype(BF16)

    acc = x_ref[...]
    k0 = 0
    for a_ref in refs[1:-3]:
        kw = a_ref.shape[1]
        acc = acc + _dot(a_ref[...], wb_ref[k0:k0 + kw, :])
        k0 += kw
    o_ref[...] = acc


def _out_proj(x, parts, w, layer, *, tm):
    t, d = x.shape
    k = w.shape[1]
    in_specs = [pl.BlockSpec((tm, d), lambda i: (i, 0))]
    in_specs += [pl.BlockSpec((tm, a.shape[1]), lambda i: (i, 0)) for a in parts]
    in_specs += [_stacked((k, d), layer, lambda i: (0, 0), pipeline_mode=pl.Buffered(1))]
    return pl.pallas_call(
        _out_proj_kernel,
        out_shape=jax.ShapeDtypeStruct((t, d), F32),
        grid=(t // tm,),
        in_specs=in_specs,
        out_specs=pl.BlockSpec((tm, d), lambda i: (i, 0)),
        scratch_shapes=[pltpu.VMEM((k, d), BF16)],
        compiler_params=_cparams(("arbitrary",)),
        name="out_proj",
    )(x, *parts, w)


def _ple_kernel(x_ref, p_ref, g_ref, wg_ref, wp_ref, fg_ref, o_ref, wgb_ref, wpb_ref, *,
                final_norm):
    @pl.when(pl.program_id(0) == 0)
    def _():
        wgb_ref[...] = wg_ref[...].astype(BF16)
        wpb_ref[...] = wp_ref[...].astype(BF16)

    x = x_ref[...]
    h = _rms(x, g_ref[...], NORM_EPS).astype(BF16)
    gate = jax.nn.sigmoid(_dot(h, wgb_ref[...]))
    y = x + _dot(p_ref[...].astype(BF16), wpb_ref[...]) * gate
    if final_norm:
        y = _rms(y, fg_ref[...], NORM_EPS)
    o_ref[...] = y


def _ple(x, p, gain, w_gate, w_proj, final_gain, layer, *, tm, final_norm):
    t, d = x.shape
    pd = p.shape[2]
    once = pl.Buffered(1)
    return pl.pallas_call(
        functools.partial(_ple_kernel, final_norm=final_norm),
        out_shape=jax.ShapeDtypeStruct((t, d), F32),
        grid=(t // tm,),
        in_specs=[
            pl.BlockSpec((tm, d), lambda i: (i, 0)),
            _stacked((tm, pd), layer, lambda i: (i, 0)),
            _stacked((1, d), layer, lambda i: (0, 0)),
            _stacked((d, d), layer, lambda i: (0, 0), pipeline_mode=once),
            _stacked((pd, d), layer, lambda i: (0, 0), pipeline_mode=once),
            pl.BlockSpec((1, d), lambda i: (0, 0)),
        ],
        out_specs=pl.BlockSpec((tm, d), lambda i: (i, 0)),
        scratch_shapes=[pltpu.VMEM((d, d), BF16), pltpu.VMEM((pd, d), BF16)],
        compiler_params=_cparams(("arbitrary",)),
        name="ple",
    )(x, p, gain, w_gate, w_proj, final_gain)


def _hgrn_kernel(q_ref, f_ref, v_ref, g_ref, lb_ref, ng_ref, o_ref, st_ref, qb_ref, gb_ref,
                 hb_ref, *, tt, hb, unroll):
    c16 = HGRN_CHUNK
    half = c16 // 2
    hd = HGRN_HEAD_DIM
    width = hb * hd
    log2e = math.log2(math.e)

    @pl.when(pl.program_id(2) == 0)
    def _():
        st_ref[...] = jnp.zeros(st_ref.shape, F32)

    lb = lb_ref[...]
    log_lb = jnp.log(lb)
    log_1m_lb = jnp.log1p(-lb)
    row = lax.broadcasted_iota(jnp.int32, (c16, width), 0)

    def rows_of(c):
        return pl.ds(pl.multiple_of(c * c16, c16), c16)

    def gates(c, lowest):
        rows = rows_of(c)
        q = q_ref[0, rows, :]
        qb_ref[rows, :] = q * jax.nn.sigmoid(q)
        ff = f_ref[0, rows, :]
        log_sig = jnp.minimum(ff, 0.0) - jnp.log(1.0 + jnp.exp(-jnp.abs(ff)))
        b = log_1m_lb + log_sig
        log_f = jnp.maximum(log_lb, b) + jnp.log(1.0 + jnp.exp(-jnp.abs(log_lb - b)))
        log_k = b - ff
        cum = log_f
        shift = 1
        while shift < c16:
            cum = cum + jnp.where(row >= shift, pltpu.roll(cum, shift, axis=0), 0.0)
            shift *= 2
        g2 = cum * log2e
        gb_ref[rows, :] = g2
        hb_ref[rows, :] = (cum - log_k) * log2e
        return jnp.minimum(lowest, jnp.minimum(g2[:half], g2[half:]))

    lowest = lax.fori_loop(0, tt // c16, gates, jnp.zeros((half, width), F32))

    ones = jnp.ones((hd, hd), BF16)
    t8 = lax.broadcasted_iota(jnp.int32, (half, hd), 0)
    tril = (lax.broadcasted_iota(jnp.int32, (c16, c16), 0)
            >= lax.broadcasted_iota(jnp.int32, (c16, c16), 1))

    def load_chunk(c, h):
        rows, cols = rows_of(c), slice(h * hd, (h + 1) * hd)
        return gb_ref[rows, cols], hb_ref[rows, cols], qb_ref[rows, cols], v_ref[0, rows, cols]

    def finish(c, h, out):
        rows, cols = rows_of(c), slice(h * hd, (h + 1) * hd)
        gate = g_ref[0, rows, cols]
        y = _rms(out, ng_ref[...], NORM_EPS) * (gate * jax.nn.sigmoid(gate))
        o_ref[0, rows, cols] = y.astype(o_ref.dtype)

    def intra_pairs(qc, g2, h2, vc):
        g_top, g_bot = g2[:half], g2[half:]
        q_top, q_bot = qc[:half], qc[half:]
        lhs = []
        for s in range(half):
            hs = h2[s:s + 1, :]
            d_top = g_top - hs
            if s > 0:
                d_top = jnp.where(t8 >= s, d_top, -jnp.inf)
            p = jnp.concatenate([q_top * jnp.exp2(d_top), q_bot * jnp.exp2(g_bot - hs)], axis=0)
            lhs.append(p.astype(BF16))
        bots = []
        for s in range(half, c16):
            d_bot = g_bot - h2[s:s + 1, :]
            if s > half:
                d_bot = jnp.where(t8 >= s - half, d_bot, -jnp.inf)
            bots.append(q_bot * jnp.exp2(d_bot))
        for j in range(0, half, 2):
            lhs.append(jnp.concatenate([bots[j], bots[j + 1]], axis=0).astype(BF16))
        red = _dot(jnp.concatenate(lhs, axis=0), ones)
        top_terms = [red[s * c16:s * c16 + half] * vc[s:s + 1, :] for s in range(half)]
        bot_terms = [red[s * c16 + half:(s + 1) * c16] * vc[s:s + 1, :] for s in range(half)]
        base = half * c16
        bot_terms += [red[base + j * half:base + (j + 1) * half] * vc[half + j:half + j + 1, :]
                      for j in range(half)]
        return jnp.concatenate([_tree_sum(top_terms), _tree_sum(bot_terms)], axis=0)

    def chunk_pairs(c, carry):
        for h in range(hb):
            g2, h2, qc, vc = load_chunk(c, h)
            state = st_ref[h]
            inter = _dot_nt((qc * jnp.exp2(g2)).astype(BF16), state.astype(BF16))
            finish(c, h, inter + intra_pairs(qc, g2, h2, vc))
            g_last = g2[c16 - 1:c16, :]
            kdec = jnp.exp2(g_last - h2).astype(BF16)
            st_ref[h] = state * jnp.exp2(g_last) + _dot_tn(vc.astype(BF16), kdec)
        return carry

    def chunks_factored(i, carry):
        items = [(u, h) for u in range(unroll) for h in range(hb)]
        q_dec, k_inv, k_dec, v_bf, decay = {}, {}, {}, {}, {}
        for u, h in items:
            g2, h2, qc, vc = load_chunk(i * unroll + u, h)
            g_last = g2[c16 - 1:c16, :]
            q_dec[u, h] = (qc * jnp.exp2(g2)).astype(BF16)
            k_inv[u, h] = jnp.exp2(-h2).astype(BF16)
            k_dec[u, h] = jnp.exp2(g_last - h2).astype(BF16)
            v_bf[u, h] = vc.astype(BF16)
            decay[u, h] = jnp.exp2(g_last)
        scores = {k: _dot_nt(q_dec[k], k_inv[k]) for k in items}
        inter = {}
        for u in range(unroll):
            for h in range(hb):
                inter[u, h] = _dot_nt(q_dec[u, h], st_ref[h].astype(BF16))
            for h in range(hb):
                st_ref[h] = st_ref[h] * decay[u, h] + _dot_tn(v_bf[u, h], k_dec[u, h])
        for u, h in items:
            causal = jnp.where(tril, scores[u, h], 0.0).astype(BF16)
            finish(i * unroll + u, h, inter[u, h] + _dot(causal, v_bf[u, h]))
        return carry

    mild = jnp.min(lowest) >= -HGRN_SAFE_BITS

    @pl.when(mild)
    def _():
        lax.fori_loop(0, tt // (c16 * unroll), chunks_factored, 0)

    @pl.when(jnp.logical_not(mild))
    def _():
        lax.fori_loop(0, tt // c16, chunk_pairs, 0)


def _hgrn_scan(proj3, lb, norm_gain, layer, *, heads, tt, hb, unroll):
    b, s, w4 = proj3.shape
    hd = HGRN_HEAD_DIM
    width = hb * hd
    ng = heads // hb

    def sec(k):
        return pl.BlockSpec((1, tt, width), lambda bi, hg, i: (bi, i, k * ng + hg))

    return pl.pallas_call(
        functools.partial(_hgrn_kernel, tt=tt, hb=hb, unroll=unroll),
        out_shape=jax.ShapeDtypeStruct((b, s, w4 // 4), BF16),
        grid=(b, ng, s // tt),
        in_specs=[sec(0), sec(1), sec(2), sec(3),
                  pl.BlockSpec((1, width), lambda bi, hg, i: (0, hg)),
                  _stacked((1, hd), layer, lambda bi, hg, i: (0, 0))],
        out_specs=pl.BlockSpec((1, tt, width), lambda bi, hg, i: (bi, i, hg)),
        scratch_shapes=[pltpu.VMEM((hb, hd, hd), F32), pltpu.VMEM((tt, width), F32),
                        pltpu.VMEM((tt, width), F32), pltpu.VMEM((tt, width), F32)],
        compiler_params=_cparams(("parallel", "parallel", "arbitrary")),
        name="hgrn_scan",
    )(proj3, proj3, proj3, proj3, lb, norm_gain)


def kernel(x, p, positions, ffn1_norm, ffn1_w_gate, ffn1_w_up, ffn1_w_down, mix_norm, ffn2_norm, ffn2_w_gate, ffn2_w_up, ffn2_w_down, ple_norm, ple_w_gate, ple_w_proj, ab_w_in, pool_w, pool_scale, diff_lambda, diff_subln, ab_w_out, hgrn_w_in, hgrn_lower_bounds, hgrn_norm, hgrn_w_out, final_norm):
    bsz, seq, d = x.shape
    depth = p.shape[0]
    t = bsz * seq
    tm_big = min(1024, t)
    tm = min(512, t)
    ts = min(512, seq)

    def gains(a):
        return a.reshape(a.shape[0], 1, a.shape[-1]).astype(F32)

    half = ROT_DIM // 2
    inv_freq = ROPE_THETA ** (-jnp.arange(0, ROT_DIM, 2, dtype=F32) / ROT_DIM)
    zeros = jnp.zeros((LANES - ROT_DIM,), F32)
    freq = jnp.concatenate([inv_freq, inv_freq, zeros]).reshape(1, LANES)
    sign = jnp.concatenate([-jnp.ones((half,), F32), jnp.ones((half,), F32), zeros]).reshape(1, LANES)
    pos = positions.reshape(t, 1).astype(F32)

    lbs = jax.nn.softmax(hgrn_lower_bounds.astype(F32), axis=0)
    lbs = jnp.cumsum(lbs, axis=0) - lbs[0]

    ffn1_g, ffn2_g, mix_g, ple_g = gains(ffn1_norm), gains(ffn2_norm), gains(mix_norm), gains(ple_norm)
    p3 = p.reshape(depth, t, p.shape[-1])
    xt = x.reshape(t, d)
    for i in range(depth):
        xt = _ffn(xt, ffn1_g, ffn1_w_gate, ffn1_w_up, ffn1_w_down, i, tm=tm_big, tf=256)
        if i % 2 == 0:
            e = i // 2
            lam_init = 0.8 - 0.6 * math.exp(-0.3 * i)
            u, qkv = _ab_proj(xt, mix_g, ab_w_in, pos, freq, sign, i, e, tm=tm_big, tn=512)
            pw = u.shape[1]
            a_out = _pool(u.reshape(bsz, seq, pw), pool_w, gains(pool_scale), e, ts=ts)
            heads = qkv.shape[1] // (3 * 2 * DIFF_HEAD_DIM)
            b_out = _diff_attention(qkv.reshape(bsz, seq, -1), diff_lambda.astype(F32),
                                    gains(diff_subln), lam_init, e, heads=heads,
                                    tq=min(2048, seq), rs=min(512, seq))
            xt = _out_proj(xt, [a_out.reshape(t, -1), b_out.reshape(t, -1)], ab_w_out, e, tm=tm)
        else:
            o = i // 2
            proj = _norm_matmul(xt, mix_g, hgrn_w_in, i, o, tm=tm_big, tn=1024)
            heads = d // HGRN_HEAD_DIM
            mixed = _hgrn_scan(proj.reshape(bsz, seq, -1), lbs[i].reshape(1, -1), gains(hgrn_norm),
                               o, heads=heads, tt=min(256, seq), hb=16, unroll=4)
            xt = _out_proj(xt, [mixed.reshape(t, -1)], hgrn_w_out, o, tm=tm)
        xt = _ffn(xt, ffn2_g, ffn2_w_gate, ffn2_w_up, ffn2_w_down, i, tm=tm_big, tf=256)
        xt = _ple(xt, p3, ple_g, ple_w_gate, ple_w_proj, final_norm.reshape(1, -1).astype(F32), i,
                  tm=tm, final_norm=(i == depth - 1))
    return xt.reshape(bsz, seq, d)
```

```python
import functools
import math

import jax
import jax.numpy as jnp
from jax import lax
from jax.experimental import pallas as pl
from jax.experimental.pallas import tpu as pltpu

F32 = jnp.float32
BF16 = jnp.bfloat16

NORM_EPS = 1e-6
SUBLN_EPS = 1e-5
POOL_WINDOWS = (2, 4, 8, 16)
POOL_HALO = 16
DIFF_HEAD_DIM = 128
ROT_DIM = DIFF_HEAD_DIM // 4
ROPE_THETA = 500000.0
HGRN_HEAD_DIM = 128
HGRN_CHUNK = 16
HGRN_BLOCK = 64
HGRN_SAFE_BITS = 100.0

LANES = 128
VMEM_LIMIT = 56 * 1024 * 1024


def _cparams(sem):
    return pltpu.CompilerParams(dimension_semantics=sem, vmem_limit_bytes=VMEM_LIMIT)


def _stacked(block, layer, index, **kwargs):
    return pl.BlockSpec((None,) + tuple(block), lambda *g: (layer,) + tuple(index(*g)), **kwargs)


def _rms(x, gain, eps):
    ms = jnp.mean(x * x, axis=-1, keepdims=True)
    return x * lax.rsqrt(ms + eps) * gain


def _dot(a, b):
    return jnp.dot(a, b, preferred_element_type=F32)


def _dot_nt(a, b):
    return lax.dot_general(a, b, (((1,), (1,)), ((), ())), preferred_element_type=F32)


def _dot_tn(a, b):
    return lax.dot_general(a, b, (((0,), (0,)), ((), ())), preferred_element_type=F32)


def _tree_sum(terms):
    while len(terms) > 1:
        terms = [terms[i] + terms[i + 1] for i in range(0, len(terms) - 1, 2)] + (
            [terms[-1]] if len(terms) % 2 else [])
    return terms[0]


def _ffn_kernel(x_ref, g_ref, wg_ref, wu_ref, wd_ref, o_ref, h_ref):
    j = pl.program_id(1)

    @pl.when(j == 0)
    def _():
        h_ref[...] = _rms(x_ref[...], g_ref[...], NORM_EPS).astype(BF16)
        o_ref[...] = jnp.zeros_like(o_ref)

    h = h_ref[...]
    a = _dot(h, wg_ref[...].astype(BF16))
    b = _dot(h, wu_ref[...].astype(BF16))
    act = (a * jax.nn.sigmoid(a) * b).astype(BF16)
    o_ref[...] += _dot(act, wd_ref[...].astype(BF16))

    @pl.when(j == pl.num_programs(1) - 1)
    def _():
        o_ref[...] = x_ref[...] + 0.5 * o_ref[...]


def _ffn(x, gain, w_gate, w_up, w_down, layer, *, tm, tf):
    t, d = x.shape
    f = w_gate.shape[2]
    return pl.pallas_call(
        _ffn_kernel,
        out_shape=jax.ShapeDtypeStruct((t, d), F32),
        grid=(t // tm, f // tf),
        in_specs=[
            pl.BlockSpec((tm, d), lambda i, j: (i, 0)),
            _stacked((1, d), layer, lambda i, j: (0, 0)),
            _stacked((d, tf), layer, lambda i, j: (0, j)),
            _stacked((d, tf), layer, lambda i, j: (0, j)),
            _stacked((tf, d), layer, lambda i, j: (j, 0)),
        ],
        out_specs=pl.BlockSpec((tm, d), lambda i, j: (i, 0)),
        scratch_shapes=[pltpu.VMEM((tm, d), BF16)],
        compiler_params=_cparams(("parallel", "arbitrary")),
        name="ffn",
    )(x, gain, w_gate, w_up, w_down)


def _norm_matmul_kernel(x_ref, g_ref, w_ref, o_ref, h_ref):
    @pl.when(pl.program_id(1) == 0)
    def _():
        h_ref[...] = _rms(x_ref[...], g_ref[...], NORM_EPS).astype(BF16)

    o_ref[...] = _dot(h_ref[...], w_ref[...].astype(BF16)).astype(o_ref.dtype)


def _norm_matmul(x, gain, w, layer, w_layer, *, tm, tn):
    t, d = x.shape
    n = w.shape[2]
    return pl.pallas_call(
        _norm_matmul_kernel,
        out_shape=jax.ShapeDtypeStruct((t, n), F32),
        grid=(t // tm, n // tn),
        in_specs=[
            pl.BlockSpec((tm, d), lambda i, j: (i, 0)),
            _stacked((1, d), layer, lambda i, j: (0, 0)),
            _stacked((d, tn), w_layer, lambda i, j: (0, j)),
        ],
        out_specs=pl.BlockSpec((tm, tn), lambda i, j: (i, j)),
        scratch_shapes=[pltpu.VMEM((tm, d), BF16)],
        compiler_params=_cparams(("parallel", "arbitrary")),
        name="norm_matmul",
    )(x, gain, w)


def _rope(y, cos, sin_signed):
    lane = lax.broadcasted_iota(jnp.int32, cos.shape, 1)
    half = ROT_DIM // 2
    outs = []
    for c in range(y.shape[1] // LANES):
        t = y[:, c * LANES:(c + 1) * LANES]
        swapped = jnp.where(lane < half, pltpu.roll(t, LANES - half, axis=1),
                            pltpu.roll(t, half, axis=1))
        outs.append(t * cos + swapped * sin_signed)
    return jnp.concatenate(outs, axis=1)


def _ab_proj_kernel(x_ref, g_ref, w_ref, pos_ref, freq_ref, sign_ref, u_ref, qkv_ref,
                    h_ref, cos_ref, sin_ref, *, q_scale, per):
    j = pl.program_id(1)
    sec = j // per

    @pl.when(j == 0)
    def _():
        h_ref[...] = _rms(x_ref[...], g_ref[...], NORM_EPS).astype(BF16)
        ang = pos_ref[...] * freq_ref[...]
        cos_ref[...] = jnp.cos(ang)
        sin_ref[...] = jnp.sin(ang) * sign_ref[...]

    y = _dot(h_ref[...], w_ref[...].astype(BF16))

    @pl.when(sec == 0)
    def _():
        u_ref[...] = y

    @pl.when(sec == 1)
    def _():
        qkv_ref[...] = (_rope(y, cos_ref[...], sin_ref[...]) * q_scale).astype(BF16)

    @pl.when(sec == 2)
    def _():
        qkv_ref[...] = _rope(y, cos_ref[...], sin_ref[...]).astype(BF16)

    @pl.when(sec == 3)
    def _():
        qkv_ref[...] = y.astype(BF16)


def _ab_proj(x, gain, w, pos, freq, sign, layer, w_layer, *, tm, tn):
    t, d = x.shape
    n = w.shape[2]
    per = n // 4 // tn
    q_scale = DIFF_HEAD_DIM ** -0.5 * math.log2(math.e)
    return pl.pallas_call(
        functools.partial(_ab_proj_kernel, q_scale=q_scale, per=per),
        out_shape=(jax.ShapeDtypeStruct((t, n // 4), F32),
                   jax.ShapeDtypeStruct((t, 3 * n // 4), BF16)),
        grid=(t // tm, 4 * per),
        in_specs=[
            pl.BlockSpec((tm, d), lambda i, j: (i, 0)),
            _stacked((1, d), layer, lambda i, j: (0, 0)),
            _stacked((d, tn), w_layer, lambda i, j: (0, j)),
            pl.BlockSpec((tm, 1), lambda i, j: (i, 0)),
            pl.BlockSpec((1, LANES), lambda i, j: (0, 0)),
            pl.BlockSpec((1, LANES), lambda i, j: (0, 0)),
        ],
        out_specs=(
            pl.BlockSpec((tm, tn), lambda i, j: (i, jnp.minimum(j, per - 1))),
            pl.BlockSpec((tm, tn), lambda i, j: (i, jnp.maximum(j - per, 0))),
        ),
        scratch_shapes=[pltpu.VMEM((tm, d), BF16), pltpu.VMEM((tm, LANES), F32),
                        pltpu.VMEM((tm, LANES), F32)],
        compiler_params=_cparams(("parallel", "arbitrary")),
        name="ab_proj",
    )(x, gain, w, pos, freq, sign)


def _pool_kernel(u_ref, w_ref, s_ref, o_ref, ext_ref, *, ts):
    i = pl.program_id(1)

    @pl.when(i == 0)
    def _():
        ext_ref[0:POOL_HALO, :] = jnp.zeros((POOL_HALO, ext_ref.shape[1]), F32)

    ext_ref[POOL_HALO:, :] = u_ref[0]
    gd = w_ref.shape[1]
    row = lax.broadcasted_iota(jnp.int32, (ts, gd), 0) + i * ts
    for g, win in enumerate(POOL_WINDOWS):
        cols = slice(g * gd, (g + 1) * gd)
        acc = ext_ref[:, cols]
        shift = 1
        while shift < win:
            acc = acc + pltpu.roll(acc, shift, axis=0)
            shift *= 2
        cnt = jnp.minimum(row + 1, win).astype(F32)
        v = u_ref[0, :, cols]
        dlt = acc[POOL_HALO:, :] / cnt - v
        y = _dot(dlt.astype(BF16), w_ref[g].astype(BF16)) * s_ref[:, cols]
        o_ref[0, :, cols] = y.astype(o_ref.dtype)
    ext_ref[0:POOL_HALO, :] = ext_ref[ts:ts + POOL_HALO, :]


def _pool(u3, pool_w, pool_scale, layer, *, ts):
    b, s, pw = u3.shape
    _, g, gd, _ = pool_w.shape
    return pl.pallas_call(
        functools.partial(_pool_kernel, ts=ts),
        out_shape=jax.ShapeDtypeStruct((b, s, pw), BF16),
        grid=(b, s // ts),
        in_specs=[
            pl.BlockSpec((1, ts, pw), lambda bi, i: (bi, i, 0)),
            _stacked((g, gd, gd), layer, lambda bi, i: (0, 0, 0)),
            _stacked((1, pw), layer, lambda bi, i: (0, 0)),
        ],
        out_specs=pl.BlockSpec((1, ts, pw), lambda bi, i: (bi, i, 0)),
        scratch_shapes=[pltpu.VMEM((ts + POOL_HALO, pw), F32)],
        compiler_params=_cparams(("parallel", "arbitrary")),
        name="pool",
    )(u3, pool_w, pool_scale)


def _attn_kernel(lam_ref, q_ref, k_ref, v_ref, sg_ref, o_ref, m_ref, l_ref, acc_ref, s_ref,
                 p_ref, a_ref, *, tq, rs, lam_init):
    dh = DIFF_HEAD_DIM
    strip = 16
    qi = pl.program_id(2)
    m_ref[...] = jnp.full(m_ref.shape, -jnp.inf, F32)
    l_ref[...] = jnp.zeros(l_ref.shape, F32)
    acc_ref[...] = jnp.zeros(acc_ref.shape, F32)

    def step(kb, masked):
        r0 = pl.multiple_of(kb * tq, tq)

        def width(r):
            return (r + 1) * rs if masked else tq

        def qk(r):
            nk = width(r)
            for c in range(2):
                comp = slice(c * dh, (c + 1) * dh)
                s_ref[r % 2, c, :, :nk] = _dot_nt(q_ref[0, r * rs:(r + 1) * rs, comp],
                                                  k_ref[0, pl.ds(r0, nk), comp])

        def softmax(r):
            nk = width(r)
            for c in range(2):
                for i in range(rs // strip):
                    lo = r * rs + i * strip
                    rows = slice(lo, lo + strip)
                    ns = min(nk, -(-(lo + strip) // LANES) * LANES) if masked else nk
                    s = s_ref[r % 2, c, i * strip:(i + 1) * strip, :ns]
                    if masked:
                        row = lax.broadcasted_iota(jnp.int32, s.shape, 0) + lo
                        col = lax.broadcasted_iota(jnp.int32, s.shape, 1)
                        s = jnp.where(col <= row, s, -jnp.inf)
                    m_prev = m_ref[c, rows]
                    m_new = jnp.maximum(m_prev, jnp.max(s, axis=-1, keepdims=True))
                    alpha = jnp.exp2(m_prev - m_new)
                    p = jnp.exp2(s - m_new)
                    part = _tree_sum([p[:, j * LANES:(j + 1) * LANES] for j in range(ns // LANES)])
                    l_ref[c, rows] = alpha * l_ref[c, rows] + part
                    m_ref[c, rows] = m_new
                    a_ref[r % 2, c, i * strip:(i + 1) * strip] = alpha
                    prow = slice(c * rs + i * strip, c * rs + (i + 1) * strip)
                    p_ref[r % 2, prow, :ns] = p.astype(BF16)
                    if ns < nk:
                        p_ref[r % 2, prow, ns:nk] = jnp.zeros((strip, nk - ns), BF16)

        def pv(r):
            nk = width(r)
            out = _dot(p_ref[r % 2, :, :nk], v_ref[0, pl.ds(r0, nk), :])
            rows = slice(r * rs, (r + 1) * rs)
            acc_ref[0, rows] = a_ref[r % 2, 0] * acc_ref[0, rows] + out[:rs]
            acc_ref[1, rows] = a_ref[r % 2, 1] * acc_ref[1, rows] + out[rs:]

        nr = tq // rs
        qk(0)
        for r in range(nr):
            if r + 1 < nr:
                qk(r + 1)
            softmax(r)
            pv(r)

    def body(kb, carry):
        step(kb, False)
        return carry

    lax.fori_loop(0, qi, body, 0)
    step(qi, True)

    lp = lam_ref[...]
    lam = (jnp.exp(jnp.sum(lp[0:1] * lp[1:2], axis=-1, keepdims=True))
           - jnp.exp(jnp.sum(lp[2:3] * lp[3:4], axis=-1, keepdims=True)) + lam_init)
    l0 = jnp.sum(l_ref[0], axis=-1, keepdims=True)
    l1 = jnp.sum(l_ref[1], axis=-1, keepdims=True)
    o = acc_ref[0] / l0 - lam * (acc_ref[1] / l1)
    o = _rms(o, sg_ref[...], SUBLN_EPS) * (1.0 - lam_init)
    o_ref[0] = o.astype(o_ref.dtype)


def _diff_attention(qkv3, lam_params, subln, lam_init, layer, *, heads, tq, rs):
    b, s, w3 = qkv3.shape
    hw = 2 * DIFF_HEAD_DIM
    return pl.pallas_call(
        functools.partial(_attn_kernel, tq=tq, rs=rs, lam_init=lam_init),
        out_shape=jax.ShapeDtypeStruct((b, s, w3 // 3), BF16),
        grid=(b, heads, s // tq),
        in_specs=[
            _stacked((4, DIFF_HEAD_DIM), layer, lambda bi, h, i: (0, 0)),
            pl.BlockSpec((1, tq, hw), lambda bi, h, i: (bi, i, h)),
            pl.BlockSpec((1, s, hw), lambda bi, h, i: (bi, 0, heads + h),
                         pipeline_mode=pl.Buffered(1)),
            pl.BlockSpec((1, s, hw), lambda bi, h, i: (bi, 0, 2 * heads + h),
                         pipeline_mode=pl.Buffered(1)),
            _stacked((1, hw), layer, lambda bi, h, i: (0, 0)),
        ],
        out_specs=pl.BlockSpec((1, tq, hw), lambda bi, h, i: (bi, i, h)),
        scratch_shapes=[pltpu.VMEM((2, tq, 1), F32), pltpu.VMEM((2, tq, LANES), F32),
                        pltpu.VMEM((2, tq, hw), F32), pltpu.VMEM((2, 2, rs, tq), F32),
                        pltpu.VMEM((2, 2 * rs, tq), BF16), pltpu.VMEM((2, 2, rs, 1), F32)],
        compiler_params=_cparams(("parallel", "parallel", "arbitrary")),
        name="diff_attention",
    )(lam_params, qkv3, qkv3, qkv3, subln)


def _out_proj_kernel(*refs):
    x_ref, w_ref, o_ref, wb_ref = refs[0], refs[-3], refs[-2], refs[-1]

    @pl.when(pl.program_id(0) == 0)
    def _():
        wb_ref[...] = w_ref[...].astype(BF16)

    acc = x_ref[...]
    k0 = 0
    for a_ref in refs[1:-3]:
        kw = a_ref.shape[1]
        acc = acc + _dot(a_ref[...], wb_ref[k0:k0 + kw, :])
        k0 += kw
    o_ref[...] = acc


def _out_proj(x, parts, w, layer, *, tm):
    t, d = x.shape
    k = w.shape[1]
    in_specs = [pl.BlockSpec((tm, d), lambda i: (i, 0))]
    in_specs += [pl.BlockSpec((tm, a.shape[1]), lambda i: (i, 0)) for a in parts]
    in_specs += [_stacked((k, d), layer, lambda i: (0, 0), pipeline_mode=pl.Buffered(1))]
    return pl.pallas_call(
        _out_proj_kernel,
        out_shape=jax.ShapeDtypeStruct((t, d), F32),
        grid=(t // tm,),
        in_specs=in_specs,
        out_specs=pl.BlockSpec((tm, d), lambda i: (i, 0)),
        scratch_shapes=[pltpu.VMEM((k, d), BF16)],
        compiler_params=_cparams(("arbitrary",)),
        name="out_proj",
    )(x, *parts, w)


def _ple_kernel(x_ref, p_ref, g_ref, wg_ref, wp_ref, fg_ref, o_ref, wgb_ref, wpb_ref, *,
                final_norm):
    @pl.when(pl.program_id(0) == 0)
    def _():
        wgb_ref[...] = wg_ref[...].astype(BF16)
        wpb_ref[...] = wp_ref[...].astype(BF16)

    x = x_ref[...]
    h = _rms(x, g_ref[...], NORM_EPS).astype(BF16)
    gate = jax.nn.sigmoid(_dot(h, wgb_ref[...]))
    y = x + _dot(p_ref[...].astype(BF16), wpb_ref[...]) * gate
    if final_norm:
        y = _rms(y, fg_ref[...], NORM_EPS)
    o_ref[...] = y


def _ple(x, p, gain, w_gate, w_proj, final_gain, layer, *, tm, final_norm):
    t, d = x.shape
    pd = p.shape[2]
    once = pl.Buffered(1)
    return pl.pallas_call(
        functools.partial(_ple_kernel, final_norm=final_norm),
        out_shape=jax.ShapeDtypeStruct((t, d), F32),
        grid=(t // tm,),
        in_specs=[
            pl.BlockSpec((tm, d), lambda i: (i, 0)),
            _stacked((tm, pd), layer, lambda i: (i, 0)),
            _stacked((1, d), layer, lambda i: (0, 0)),
            _stacked((d, d), layer, lambda i: (0, 0), pipeline_mode=once),
            _stacked((pd, d), layer, lambda i: (0, 0), pipeline_mode=once),
            pl.BlockSpec((1, d), lambda i: (0, 0)),
        ],
        out_specs=pl.BlockSpec((tm, d), lambda i: (i, 0)),
        scratch_shapes=[pltpu.VMEM((d, d), BF16), pltpu.VMEM((pd, d), BF16)],
        compiler_params=_cparams(("arbitrary",)),
        name="ple",
    )(x, p, gain, w_gate, w_proj, final_gain)


def _hgrn_kernel(q_ref, f_ref, v_ref, g_ref, lb_ref, ng_ref, o_ref, st_ref, qb_ref, gb_ref,
                 hb_ref, ob_ref, *, tt, hb):
    c16 = HGRN_CHUNK
    blk = HGRN_BLOCK
    per = blk // c16
    half = c16 // 2
    hd = HGRN_HEAD_DIM
    width = hb * hd
    log2e = math.log2(math.e)

    @pl.when(pl.program_id(2) == 0)
    def _():
        st_ref[...] = jnp.zeros(st_ref.shape, F32)

    lb = lb_ref[...]
    log_lb = jnp.log(lb)
    log_1m_lb = jnp.log1p(-lb)
    row = lax.broadcasted_iota(jnp.int32, (c16, width), 0)

    def rows_of(c, n=c16):
        return pl.ds(pl.multiple_of(c * n, n), n)

    def gates(i, lowest):
        off = jnp.zeros((1, width), F32)
        for j in range(per):
            rows = rows_of(i * per + j)
            q = q_ref[0, rows, :]
            qb_ref[rows, :] = q * jax.nn.sigmoid(q)
            ff = f_ref[0, rows, :]
            log_sig = jnp.minimum(ff, 0.0) - jnp.log(1.0 + jnp.exp(-jnp.abs(ff)))
            b = log_1m_lb + log_sig
            log_f = jnp.maximum(log_lb, b) + jnp.log(1.0 + jnp.exp(-jnp.abs(log_lb - b)))
            log_k = b - ff
            cum = log_f
            shift = 1
            while shift < c16:
                cum = cum + jnp.where(row >= shift, pltpu.roll(cum, shift, axis=0), 0.0)
                shift *= 2
            g2 = cum * log2e
            gb_ref[rows, :] = g2
            hb_ref[rows, :] = (log_k - cum) * log2e
            ob_ref[rows, :] = jnp.broadcast_to(off, (c16, width))
            off = off + g2[c16 - 1:c16, :]
        return jnp.minimum(lowest, off)

    lowest = lax.fori_loop(0, tt // blk, gates, jnp.zeros((1, width), F32))

    def finish(rows, h, out):
        cols = slice(h * hd, (h + 1) * hd)
        gate = g_ref[0, rows, cols]
        y = _rms(out, ng_ref[...], NORM_EPS) * (gate * jax.nn.sigmoid(gate))
        o_ref[0, rows, cols] = y.astype(o_ref.dtype)

    ones = jnp.ones((hd, hd), BF16)
    t8 = lax.broadcasted_iota(jnp.int32, (half, hd), 0)

    def intra_pairs(qc, g2, n2, vc):
        g_top, g_bot = g2[:half], g2[half:]
        q_top, q_bot = qc[:half], qc[half:]
        lhs = []
        for s in range(half):
            ns = n2[s:s + 1, :]
            d_top = g_top + ns
            if s > 0:
                d_top = jnp.where(t8 >= s, d_top, -jnp.inf)
            p = jnp.concatenate([q_top * jnp.exp2(d_top), q_bot * jnp.exp2(g_bot + ns)], axis=0)
            lhs.append(p.astype(BF16))
        bots = []
        for s in range(half, c16):
            d_bot = g_bot + n2[s:s + 1, :]
            if s > half:
                d_bot = jnp.where(t8 >= s - half, d_bot, -jnp.inf)
            bots.append(q_bot * jnp.exp2(d_bot))
        for j in range(0, half, 2):
            lhs.append(jnp.concatenate([bots[j], bots[j + 1]], axis=0).astype(BF16))
        red = _dot(jnp.concatenate(lhs, axis=0), ones)
        top_terms = [red[s * c16:s * c16 + half] * vc[s:s + 1, :] for s in range(half)]
        bot_terms = [red[s * c16 + half:(s + 1) * c16] * vc[s:s + 1, :] for s in range(half)]
        base = half * c16
        bot_terms += [red[base + j * half:base + (j + 1) * half] * vc[half + j:half + j + 1, :]
                      for j in range(half)]
        return jnp.concatenate([_tree_sum(top_terms), _tree_sum(bot_terms)], axis=0)

    def step_pairs(c, carry):
        rows = rows_of(c)
        for h in range(hb):
            cols = slice(h * hd, (h + 1) * hd)
            g2, n2, qc, vc = gb_ref[rows, cols], hb_ref[rows, cols], qb_ref[rows, cols], v_ref[0, rows, cols]
            state = st_ref[h]
            inter = _dot_nt((qc * jnp.exp2(g2)).astype(BF16), state.astype(BF16))
            finish(rows, h, inter + intra_pairs(qc, g2, n2, vc))
            g_last = g2[c16 - 1:c16, :]
            k_dec = jnp.exp2(g_last + n2).astype(BF16)
            st_ref[h] = state * jnp.exp2(g_last) + _dot_tn(vc.astype(BF16), k_dec)
        return carry

    tril = (lax.broadcasted_iota(jnp.int32, (blk, blk), 0)
            >= lax.broadcasted_iota(jnp.int32, (blk, blk), 1))

    def block_factored(i, carry):
        rows = rows_of(i, blk)
        q_dec, k_inv, k_dec, v_bf, decay = [], [], [], [], []
        for h in range(hb):
            cols = slice(h * hd, (h + 1) * hd)
            off = ob_ref[rows, cols]
            g = gb_ref[rows, cols] + off
            n = hb_ref[rows, cols] - off
            g_last = g[blk - 1:blk, :]
            q_dec.append((qb_ref[rows, cols] * jnp.exp2(g)).astype(BF16))
            k_inv.append(jnp.exp2(n).astype(BF16))
            k_dec.append(jnp.exp2(g_last + n).astype(BF16))
            v_bf.append(v_ref[0, rows, cols].astype(BF16))
            decay.append(jnp.exp2(g_last))
        scores = [_dot_nt(q_dec[h], k_inv[h]) for h in range(hb)]
        inter = [_dot_nt(q_dec[h], st_ref[h].astype(BF16)) for h in range(hb)]
        for h in range(hb):
            st_ref[h] = st_ref[h] * decay[h] + _dot_tn(v_bf[h], k_dec[h])
        for h in range(hb):
            causal = jnp.where(tril, scores[h], 0.0).astype(BF16)
            finish(rows, h, inter[h] + _dot(causal, v_bf[h]))
        return carry

    mild = jnp.min(lowest) >= -HGRN_SAFE_BITS

    @pl.when(mild)
    def _():
        lax.fori_loop(0, tt // blk, block_factored, 0)

    @pl.when(jnp.logical_not(mild))
    def _():
        lax.fori_loop(0, tt // c16, step_pairs, 0)


def _hgrn_scan(proj3, lb, norm_gain, layer, *, heads, tt, hb):
    b, s, w4 = proj3.shape
    hd = HGRN_HEAD_DIM
    width = hb * hd
    ng = heads // hb

    def sec(k):
        return pl.BlockSpec((1, tt, width), lambda bi, hg, i: (bi, i, k * ng + hg))

    return pl.pallas_call(
        functools.partial(_hgrn_kernel, tt=tt, hb=hb),
        out_shape=jax.ShapeDtypeStruct((b, s, w4 // 4), BF16),
        grid=(b, ng, s // tt),
        in_specs=[sec(0), sec(1), sec(2), sec(3),
                  pl.BlockSpec((1, width), lambda bi, hg, i: (0, hg)),
                  _stacked((1, hd), layer, lambda bi, hg, i: (0, 0))],
        out_specs=pl.BlockSpec((1, tt, width), lambda bi, hg, i: (bi, i, hg)),
        scratch_shapes=[pltpu.VMEM((hb, hd, hd), F32), pltpu.VMEM((tt, width), F32),
                        pltpu.VMEM((tt, width), F32), pltpu.VMEM((tt, width), F32),
                        pltpu.VMEM((tt, width), F32)],
        compiler_params=_cparams(("parallel", "parallel", "arbitrary")),
        name="hgrn_scan",
    )(proj3, proj3, proj3, proj3, lb, norm_gain)


def kernel(x, p, positions, ffn1_norm, ffn1_w_gate, ffn1_w_up, ffn1_w_down, mix_norm, ffn2_norm, ffn2_w_gate, ffn2_w_up, ffn2_w_down, ple_norm, ple_w_gate, ple_w_proj, ab_w_in, pool_w, pool_scale, diff_lambda, diff_subln, ab_w_out, hgrn_w_in, hgrn_lower_bounds, hgrn_norm, hgrn_w_out, final_norm):
    bsz, seq, d = x.shape
    depth = p.shape[0]
    t = bsz * seq
    tm_big = min(1024, t)
    tm = min(512, t)
    ts = min(512, seq)

    def gains(a):
        return a.reshape(a.shape[0], 1, a.shape[-1]).astype(F32)

    half = ROT_DIM // 2
    inv_freq = ROPE_THETA ** (-jnp.arange(0, ROT_DIM, 2, dtype=F32) / ROT_DIM)
    zeros = jnp.zeros((LANES - ROT_DIM,), F32)
    freq = jnp.concatenate([inv_freq, inv_freq, zeros]).reshape(1, LANES)
    sign = jnp.concatenate([-jnp.ones((half,), F32), jnp.ones((half,), F32), zeros]).reshape(1, LANES)
    pos = positions.reshape(t, 1).astype(F32)

    lbs = jax.nn.softmax(hgrn_lower_bounds.astype(F32), axis=0)
    lbs = jnp.cumsum(lbs, axis=0) - lbs[0]

    ffn1_g, ffn2_g, mix_g, ple_g = gains(ffn1_norm), gains(ffn2_norm), gains(mix_norm), gains(ple_norm)
    p3 = p.reshape(depth, t, p.shape[-1])
    xt = x.reshape(t, d)
    for i in range(depth):
        xt = _ffn(xt, ffn1_g, ffn1_w_gate, ffn1_w_up, ffn1_w_down, i, tm=tm_big, tf=256)
        if i % 2 == 0:
            e = i // 2
            lam_init = 0.8 - 0.6 * math.exp(-0.3 * i)
            u, qkv = _ab_proj(xt, mix_g, ab_w_in, pos, freq, sign, i, e, tm=tm_big, tn=512)
            pw = u.shape[1]
            a_out = _pool(u.reshape(bsz, seq, pw), pool_w, gains(pool_scale), e, ts=ts)
            heads = qkv.shape[1] // (3 * 2 * DIFF_HEAD_DIM)
            b_out = _diff_attention(qkv.reshape(bsz, seq, -1), diff_lambda.astype(F32),
                                    gains(diff_subln), lam_init, e, heads=heads,
                                    tq=min(2048, seq), rs=min(512, seq))
            xt = _out_proj(xt, [a_out.reshape(t, -1), b_out.reshape(t, -1)], ab_w_out, e, tm=tm)
        else:
            o = i // 2
            proj = _norm_matmul(xt, mix_g, hgrn_w_in, i, o, tm=tm_big, tn=1024)
            heads = d // HGRN_HEAD_DIM
            mixed = _hgrn_scan(proj.reshape(bsz, seq, -1), lbs[i].reshape(1, -1), gains(hgrn_norm),
                               o, heads=heads, tt=min(256, seq), hb=16)
            xt = _out_proj(xt, [mixed.reshape(t, -1)], hgrn_w_out, o, tm=tm)
        xt = _ffn(xt, ffn2_g, ffn2_w_gate, ffn2_w_up, ffn2_w_down, i, tm=tm_big, tf=256)
        xt = _ple(xt, p3, ple_g, ple_w_gate, ple_w_proj, final_norm.reshape(1, -1).astype(F32), i,
                  tm=tm, final_norm=(i == depth - 1))
    return xt.reshape(bsz, seq, d)
```

```python
import functools
import math

import jax
import jax.numpy as jnp
from jax import lax
from jax.experimental import pallas as pl
from jax.experimental.pallas import tpu as pltpu

F32 = jnp.float32
BF16 = jnp.bfloat16

NORM_EPS = 1e-6
SUBLN_EPS = 1e-5
POOL_WINDOWS = (2, 4, 8, 16)
POOL_HALO = 16
DIFF_HEAD_DIM = 128
ROT_DIM = DIFF_HEAD_DIM // 4
ROPE_THETA = 500000.0
HGRN_HEAD_DIM = 128
HGRN_CHUNK = 16
HGRN_BLOCK = 64
HGRN_SAFE_BITS = 100.0

LANES = 128
VMEM_LIMIT = 56 * 1024 * 1024


def _cparams(sem):
    return pltpu.CompilerParams(dimension_semantics=sem, vmem_limit_bytes=VMEM_LIMIT)


def _stacked(block, layer, index, **kwargs):
    return pl.BlockSpec((None,) + tuple(block), lambda *g: (layer,) + tuple(index(*g)), **kwargs)


def _rms(x, gain, eps):
    ms = jnp.mean(x * x, axis=-1, keepdims=True)
    return x * lax.rsqrt(ms + eps) * gain


def _silu(x):
    h = 0.5 * x
    return h + h * jnp.tanh(h)


def _dot(a, b):
    return jnp.dot(a, b, preferred_element_type=F32)


def _dot_nt(a, b):
    return lax.dot_general(a, b, (((1,), (1,)), ((), ())), preferred_element_type=F32)


def _dot_tn(a, b):
    return lax.dot_general(a, b, (((0,), (0,)), ((), ())), preferred_element_type=F32)


def _tree_sum(terms):
    while len(terms) > 1:
        terms = [terms[i] + terms[i + 1] for i in range(0, len(terms) - 1, 2)] + (
            [terms[-1]] if len(terms) % 2 else [])
    return terms[0]


def _ffn_kernel(x_ref, g_ref, wg_ref, wu_ref, wd_ref, o_ref, h_ref):
    j = pl.program_id(1)

    @pl.when(j == 0)
    def _():
        h_ref[...] = _rms(x_ref[...], g_ref[...], NORM_EPS).astype(BF16)
        o_ref[...] = jnp.zeros_like(o_ref)

    h = h_ref[...]
    a = _dot(h, wg_ref[...].astype(BF16))
    b = _dot(h, wu_ref[...].astype(BF16))
    act = (_silu(a) * b).astype(BF16)
    o_ref[...] += _dot(act, wd_ref[...].astype(BF16))

    @pl.when(j == pl.num_programs(1) - 1)
    def _():
        o_ref[...] = x_ref[...] + 0.5 * o_ref[...]


def _ffn(x, gain, w_gate, w_up, w_down, layer, *, tm, tf):
    t, d = x.shape
    f = w_gate.shape[2]
    return pl.pallas_call(
        _ffn_kernel,
        out_shape=jax.ShapeDtypeStruct((t, d), F32),
        grid=(t // tm, f // tf),
        in_specs=[
            pl.BlockSpec((tm, d), lambda i, j: (i, 0)),
            _stacked((1, d), layer, lambda i, j: (0, 0)),
            _stacked((d, tf), layer, lambda i, j: (0, j)),
            _stacked((d, tf), layer, lambda i, j: (0, j)),
            _stacked((tf, d), layer, lambda i, j: (j, 0)),
        ],
        out_specs=pl.BlockSpec((tm, d), lambda i, j: (i, 0)),
        scratch_shapes=[pltpu.VMEM((tm, d), BF16)],
        compiler_params=_cparams(("parallel", "arbitrary")),
        name="ffn",
    )(x, gain, w_gate, w_up, w_down)


def _norm_matmul_kernel(x_ref, g_ref, w_ref, o_ref, h_ref):
    @pl.when(pl.program_id(1) == 0)
    def _():
        h_ref[...] = _rms(x_ref[...], g_ref[...], NORM_EPS).astype(BF16)

    o_ref[...] = _dot(h_ref[...], w_ref[...].astype(BF16)).astype(o_ref.dtype)


def _norm_matmul(x, gain, w, layer, w_layer, *, tm, tn):
    t, d = x.shape
    n = w.shape[2]
    return pl.pallas_call(
        _norm_matmul_kernel,
        out_shape=jax.ShapeDtypeStruct((t, n), F32),
        grid=(t // tm, n // tn),
        in_specs=[
            pl.BlockSpec((tm, d), lambda i, j: (i, 0)),
            _stacked((1, d), layer, lambda i, j: (0, 0)),
            _stacked((d, tn), w_layer, lambda i, j: (0, j)),
        ],
        out_specs=pl.BlockSpec((tm, tn), lambda i, j: (i, j)),
        scratch_shapes=[pltpu.VMEM((tm, d), BF16)],
        compiler_params=_cparams(("parallel", "arbitrary")),
        name="norm_matmul",
    )(x, gain, w)


def _rope(y, cos, sin_signed):
    lane = lax.broadcasted_iota(jnp.int32, cos.shape, 1)
    half = ROT_DIM // 2
    outs = []
    for c in range(y.shape[1] // LANES):
        t = y[:, c * LANES:(c + 1) * LANES]
        swapped = jnp.where(lane < half, pltpu.roll(t, LANES - half, axis=1),
                            pltpu.roll(t, half, axis=1))
        outs.append(t * cos + swapped * sin_signed)
    return jnp.concatenate(outs, axis=1)


def _ab_proj_kernel(x_ref, g_ref, w_ref, pos_ref, freq_ref, sign_ref, u_ref, qkv_ref,
                    h_ref, cos_ref, sin_ref, *, q_scale, per):
    j = pl.program_id(1)
    sec = j // per

    @pl.when(j == 0)
    def _():
        h_ref[...] = _rms(x_ref[...], g_ref[...], NORM_EPS).astype(BF16)
        ang = pos_ref[...] * freq_ref[...]
        cos_ref[...] = jnp.cos(ang)
        sin_ref[...] = jnp.sin(ang) * sign_ref[...]

    y = _dot(h_ref[...], w_ref[...].astype(BF16))

    @pl.when(sec == 0)
    def _():
        u_ref[...] = y

    @pl.when(sec == 1)
    def _():
        qkv_ref[...] = (_rope(y, cos_ref[...], sin_ref[...]) * q_scale).astype(BF16)

    @pl.when(sec == 2)
    def _():
        qkv_ref[...] = _rope(y, cos_ref[...], sin_ref[...]).astype(BF16)

    @pl.when(sec == 3)
    def _():
        qkv_ref[...] = y.astype(BF16)


def _ab_proj(x, gain, w, pos, freq, sign, layer, w_layer, *, tm, tn):
    t, d = x.shape
    n = w.shape[2]
    per = n // 4 // tn
    q_scale = DIFF_HEAD_DIM ** -0.5 * math.log2(math.e)
    return pl.pallas_call(
        functools.partial(_ab_proj_kernel, q_scale=q_scale, per=per),
        out_shape=(jax.ShapeDtypeStruct((t, n // 4), F32),
                   jax.ShapeDtypeStruct((t, 3 * n // 4), BF16)),
        grid=(t // tm, 4 * per),
        in_specs=[
            pl.BlockSpec((tm, d), lambda i, j: (i, 0)),
            _stacked((1, d), layer, lambda i, j: (0, 0)),
            _stacked((d, tn), w_layer, lambda i, j: (0, j)),
            pl.BlockSpec((tm, 1), lambda i, j: (i, 0)),
            pl.BlockSpec((1, LANES), lambda i, j: (0, 0)),
            pl.BlockSpec((1, LANES), lambda i, j: (0, 0)),
        ],
        out_specs=(
            pl.BlockSpec((tm, tn), lambda i, j: (i, jnp.minimum(j, per - 1))),
            pl.BlockSpec((tm, tn), lambda i, j: (i, jnp.maximum(j - per, 0))),
        ),
        scratch_shapes=[pltpu.VMEM((tm, d), BF16), pltpu.VMEM((tm, LANES), F32),
                        pltpu.VMEM((tm, LANES), F32)],
        compiler_params=_cparams(("parallel", "arbitrary")),
        name="ab_proj",
    )(x, gain, w, pos, freq, sign)


def _pool_kernel(u_ref, w_ref, s_ref, o_ref, ext_ref, *, ts):
    i = pl.program_id(1)

    @pl.when(i == 0)
    def _():
        ext_ref[0:POOL_HALO, :] = jnp.zeros((POOL_HALO, ext_ref.shape[1]), F32)

    ext_ref[POOL_HALO:, :] = u_ref[0]
    gd = w_ref.shape[1]
    row = lax.broadcasted_iota(jnp.int32, (ts, gd), 0) + i * ts
    for g, win in enumerate(POOL_WINDOWS):
        cols = slice(g * gd, (g + 1) * gd)
        acc = ext_ref[:, cols]
        shift = 1
        while shift < win:
            acc = acc + pltpu.roll(acc, shift, axis=0)
            shift *= 2
        cnt = jnp.minimum(row + 1, win).astype(F32)
        v = u_ref[0, :, cols]
        dlt = acc[POOL_HALO:, :] / cnt - v
        y = _dot(dlt.astype(BF16), w_ref[g].astype(BF16)) * s_ref[:, cols]
        o_ref[0, :, cols] = y.astype(o_ref.dtype)
    ext_ref[0:POOL_HALO, :] = ext_ref[ts:ts + POOL_HALO, :]


def _pool(u3, pool_w, pool_scale, layer, *, ts):
    b, s, pw = u3.shape
    _, g, gd, _ = pool_w.shape
    return pl.pallas_call(
        functools.partial(_pool_kernel, ts=ts),
        out_shape=jax.ShapeDtypeStruct((b, s, pw), BF16),
        grid=(b, s // ts),
        in_specs=[
            pl.BlockSpec((1, ts, pw), lambda bi, i: (bi, i, 0)),
            _stacked((g, gd, gd), layer, lambda bi, i: (0, 0, 0)),
            _stacked((1, pw), layer, lambda bi, i: (0, 0)),
        ],
        out_specs=pl.BlockSpec((1, ts, pw), lambda bi, i: (bi, i, 0)),
        scratch_shapes=[pltpu.VMEM((ts + POOL_HALO, pw), F32)],
        compiler_params=_cparams(("parallel", "arbitrary")),
        name="pool",
    )(u3, pool_w, pool_scale)


def _attn_kernel(lam_ref, q_ref, k_ref, v_ref, sg_ref, o_ref, m_ref, l_ref, acc_ref, s_ref,
                 p_ref, a_ref, *, tq, rs, lam_init):
    dh = DIFF_HEAD_DIM
    strip = 16
    qi = pl.program_id(2)
    m_ref[...] = jnp.full(m_ref.shape, -jnp.inf, F32)
    l_ref[...] = jnp.zeros(l_ref.shape, F32)
    acc_ref[...] = jnp.zeros(acc_ref.shape, F32)

    def step(kb, masked):
        r0 = pl.multiple_of(kb * tq, tq)

        def width(r):
            return (r + 1) * rs if masked else tq

        def qk(r):
            nk = width(r)
            for c in range(2):
                comp = slice(c * dh, (c + 1) * dh)
                s_ref[r % 2, c, :, :nk] = _dot_nt(q_ref[0, r * rs:(r + 1) * rs, comp],
                                                  k_ref[0, pl.ds(r0, nk), comp])

        def softmax(r):
            nk = width(r)
            for c in range(2):
                for i in range(rs // strip):
                    lo = r * rs + i * strip
                    rows = slice(lo, lo + strip)
                    ns = min(nk, -(-(lo + strip) // LANES) * LANES) if masked else nk
                    s = s_ref[r % 2, c, i * strip:(i + 1) * strip, :ns]
                    if masked:
                        row = lax.broadcasted_iota(jnp.int32, s.shape, 0) + lo
                        col = lax.broadcasted_iota(jnp.int32, s.shape, 1)
                        s = jnp.where(col <= row, s, -jnp.inf)
                    m_prev = m_ref[c, rows]
                    m_new = jnp.maximum(m_prev, jnp.max(s, axis=-1, keepdims=True))
                    alpha = jnp.exp2(m_prev - m_new)
                    p = jnp.exp2(s - m_new)
                    part = _tree_sum([p[:, j * LANES:(j + 1) * LANES] for j in range(ns // LANES)])
                    l_ref[c, rows] = alpha * l_ref[c, rows] + part
                    m_ref[c, rows] = m_new
                    a_ref[r % 2, c, i * strip:(i + 1) * strip] = alpha
                    prow = slice(c * rs + i * strip, c * rs + (i + 1) * strip)
                    p_ref[r % 2, prow, :ns] = p.astype(BF16)
                    if ns < nk:
                        p_ref[r % 2, prow, ns:nk] = jnp.zeros((strip, nk - ns), BF16)

        def pv(r):
            nk = width(r)
            out = _dot(p_ref[r % 2, :, :nk], v_ref[0, pl.ds(r0, nk), :])
            rows = slice(r * rs, (r + 1) * rs)
            acc_ref[0, rows] = a_ref[r % 2, 0] * acc_ref[0, rows] + out[:rs]
            acc_ref[1, rows] = a_ref[r % 2, 1] * acc_ref[1, rows] + out[rs:]

        nr = tq // rs
        qk(0)
        for r in range(nr):
            if r + 1 < nr:
                qk(r + 1)
            softmax(r)
            pv(r)

    def body(kb, carry):
        step(kb, False)
        return carry

    lax.fori_loop(0, qi, body, 0)
    step(qi, True)

    lp = lam_ref[...]
    lam = (jnp.exp(jnp.sum(lp[0:1] * lp[1:2], axis=-1, keepdims=True))
           - jnp.exp(jnp.sum(lp[2:3] * lp[3:4], axis=-1, keepdims=True)) + lam_init)
    l0 = jnp.sum(l_ref[0], axis=-1, keepdims=True)
    l1 = jnp.sum(l_ref[1], axis=-1, keepdims=True)
    o = acc_ref[0] / l0 - lam * (acc_ref[1] / l1)
    o = _rms(o, sg_ref[...], SUBLN_EPS) * (1.0 - lam_init)
    o_ref[0] = o.astype(o_ref.dtype)


def _diff_attention(qkv3, lam_params, subln, lam_init, layer, *, heads, tq, rs):
    b, s, w3 = qkv3.shape
    hw = 2 * DIFF_HEAD_DIM
    return pl.pallas_call(
        functools.partial(_attn_kernel, tq=tq, rs=rs, lam_init=lam_init),
        out_shape=jax.ShapeDtypeStruct((b, s, w3 // 3), BF16),
        grid=(b, heads, s // tq),
        in_specs=[
            _stacked((4, DIFF_HEAD_DIM), layer, lambda bi, h, i: (0, 0)),
            pl.BlockSpec((1, tq, hw), lambda bi, h, i: (bi, i, h)),
            pl.BlockSpec((1, s, hw), lambda bi, h, i: (bi, 0, heads + h),
                         pipeline_mode=pl.Buffered(1)),
            pl.BlockSpec((1, s, hw), lambda bi, h, i: (bi, 0, 2 * heads + h),
                         pipeline_mode=pl.Buffered(1)),
            _stacked((1, hw), layer, lambda bi, h, i: (0, 0)),
        ],
        out_specs=pl.BlockSpec((1, tq, hw), lambda bi, h, i: (bi, i, h)),
        scratch_shapes=[pltpu.VMEM((2, tq, 1), F32), pltpu.VMEM((2, tq, LANES), F32),
                        pltpu.VMEM((2, tq, hw), F32), pltpu.VMEM((2, 2, rs, tq), F32),
                        pltpu.VMEM((2, 2 * rs, tq), BF16), pltpu.VMEM((2, 2, rs, 1), F32)],
        compiler_params=_cparams(("parallel", "parallel", "arbitrary")),
        name="diff_attention",
    )(lam_params, qkv3, qkv3, qkv3, subln)


def _out_proj_kernel(*refs):
    x_ref, w_ref, o_ref, wb_ref = refs[0], refs[-3], refs[-2], refs[-1]

    @pl.when(pl.program_id(0) == 0)
    def _():
        wb_ref[...] = w_ref[...].astype(BF16)

    acc = x_ref[...]
    k0 = 0
    for a_ref in refs[1:-3]:
        kw = a_ref.shape[1]
        acc = acc + _dot(a_ref[...], wb_ref[k0:k0 + kw, :])
        k0 += kw
    o_ref[...] = acc


def _out_proj(x, parts, w, layer, *, tm):
    t, d = x.shape
    k = w.shape[1]
    in_specs = [pl.BlockSpec((tm, d), lambda i: (i, 0))]
    in_specs += [pl.BlockSpec((tm, a.shape[1]), lambda i: (i, 0)) for a in parts]
    in_specs += [_stacked((k, d), layer, lambda i: (0, 0), pipeline_mode=pl.Buffered(1))]
    return pl.pallas_call(
        _out_proj_kernel,
        out_shape=jax.ShapeDtypeStruct((t, d), F32),
        grid=(t // tm,),
        in_specs=in_specs,
        out_specs=pl.BlockSpec((tm, d), lambda i: (i, 0)),
        scratch_shapes=[pltpu.VMEM((k, d), BF16)],
        compiler_params=_cparams(("arbitrary",)),
        name="out_proj",
    )(x, *parts, w)


def _ple_kernel(x_ref, p_ref, g_ref, wg_ref, wp_ref, fg_ref, o_ref, wgb_ref, wpb_ref, *,
                final_norm):
    @pl.when(pl.program_id(0) == 0)
    def _():
        wgb_ref[...] = wg_ref[...].astype(BF16)
        wpb_ref[...] = wp_ref[...].astype(BF16)

    x = x_ref[...]
    h = _rms(x, g_ref[...], NORM_EPS).astype(BF16)
    gate = jax.nn.sigmoid(_dot(h, wgb_ref[...]))
    y = x + _dot(p_ref[...].astype(BF16), wpb_ref[...]) * gate
    if final_norm:
        y = _rms(y, fg_ref[...], NORM_EPS)
    o_ref[...] = y


def _ple(x, p, gain, w_gate, w_proj, final_gain, layer, *, tm, final_norm):
    t, d = x.shape
    pd = p.shape[2]
    once = pl.Buffered(1)
    return pl.pallas_call(
        functools.partial(_ple_kernel, final_norm=final_norm),
        out_shape=jax.ShapeDtypeStruct((t, d), F32),
        grid=(t // tm,),
        in_specs=[
            pl.BlockSpec((tm, d), lambda i: (i, 0)),
            _stacked((tm, pd), layer, lambda i: (i, 0)),
            _stacked((1, d), layer, lambda i: (0, 0)),
            _stacked((d, d), layer, lambda i: (0, 0), pipeline_mode=once),
            _stacked((pd, d), layer, lambda i: (0, 0), pipeline_mode=once),
            pl.BlockSpec((1, d), lambda i: (0, 0)),
        ],
        out_specs=pl.BlockSpec((tm, d), lambda i: (i, 0)),
        scratch_shapes=[pltpu.VMEM((d, d), BF16), pltpu.VMEM((pd, d), BF16)],
        compiler_params=_cparams(("arbitrary",)),
        name="ple",
    )(x, p, gain, w_gate, w_proj, final_gain)


def _hgrn_kernel(q_ref, f_ref, v_ref, g_ref, lb_ref, ng_ref, o_ref, st_ref, qb_ref, gb_ref,
                 hb_ref, ob_ref, *, tt, hb):
    c16 = HGRN_CHUNK
    blk = HGRN_BLOCK
    per = blk // c16
    half = c16 // 2
    hd = HGRN_HEAD_DIM
    width = hb * hd
    log2e = math.log2(math.e)

    @pl.when(pl.program_id(2) == 0)
    def _():
        st_ref[...] = jnp.zeros(st_ref.shape, F32)

    lb = lb_ref[...]
    log_lb = jnp.log(lb)
    log_1m_lb = jnp.log1p(-lb)
    row = lax.broadcasted_iota(jnp.int32, (c16, width), 0)

    def rows_of(c, n=c16):
        return pl.ds(pl.multiple_of(c * n, n), n)

    def gates(i, lowest):
        off = jnp.zeros((1, width), F32)
        for j in range(per):
            rows = rows_of(i * per + j)
            q = q_ref[0, rows, :]
            qb_ref[rows, :] = _silu(q)
            ff = f_ref[0, rows, :]
            log_sig = jnp.minimum(ff, 0.0) - jnp.log(1.0 + jnp.exp(-jnp.abs(ff)))
            b = log_1m_lb + log_sig
            log_f = jnp.maximum(log_lb, b) + jnp.log(1.0 + jnp.exp(-jnp.abs(log_lb - b)))
            log_k = b - ff
            cum = log_f
            shift = 1
            while shift < c16:
                cum = cum + jnp.where(row >= shift, pltpu.roll(cum, shift, axis=0), 0.0)
                shift *= 2
            g2 = cum * log2e
            gb_ref[rows, :] = g2
            hb_ref[rows, :] = (log_k - cum) * log2e
            ob_ref[rows, :] = jnp.broadcast_to(off, (c16, width))
            off = off + g2[c16 - 1:c16, :]
        return jnp.minimum(lowest, off)

    lowest = lax.fori_loop(0, tt // blk, gates, jnp.zeros((1, width), F32))

    def finish(rows, h, out):
        cols = slice(h * hd, (h + 1) * hd)
        gate = g_ref[0, rows, cols]
        y = _rms(out, ng_ref[...], NORM_EPS) * _silu(gate)
        o_ref[0, rows, cols] = y.astype(o_ref.dtype)

    ones = jnp.ones((hd, hd), BF16)
    t8 = lax.broadcasted_iota(jnp.int32, (half, hd), 0)

    def intra_pairs(qc, g2, n2, vc):
        g_top, g_bot = g2[:half], g2[half:]
        q_top, q_bot = qc[:half], qc[half:]
        lhs = []
        for s in range(half):
            ns = n2[s:s + 1, :]
            d_top = g_top + ns
            if s > 0:
                d_top = jnp.where(t8 >= s, d_top, -jnp.inf)
            p = jnp.concatenate([q_top * jnp.exp2(d_top), q_bot * jnp.exp2(g_bot + ns)], axis=0)
            lhs.append(p.astype(BF16))
        bots = []
        for s in range(half, c16):
            d_bot = g_bot + n2[s:s + 1, :]
            if s > half:
                d_bot = jnp.where(t8 >= s - half, d_bot, -jnp.inf)
            bots.append(q_bot * jnp.exp2(d_bot))
        for j in range(0, half, 2):
            lhs.append(jnp.concatenate([bots[j], bots[j + 1]], axis=0).astype(BF16))
        red = _dot(jnp.concatenate(lhs, axis=0), ones)
        top_terms = [red[s * c16:s * c16 + half] * vc[s:s + 1, :] for s in range(half)]
        bot_terms = [red[s * c16 + half:(s + 1) * c16] * vc[s:s + 1, :] for s in range(half)]
        base = half * c16
        bot_terms += [red[base + j * half:base + (j + 1) * half] * vc[half + j:half + j + 1, :]
                      for j in range(half)]
        return jnp.concatenate([_tree_sum(top_terms), _tree_sum(bot_terms)], axis=0)

    def step_pairs(c, carry):
        rows = rows_of(c)
        for h in range(hb):
            cols = slice(h * hd, (h + 1) * hd)
            g2, n2, qc, vc = gb_ref[rows, cols], hb_ref[rows, cols], qb_ref[rows, cols], v_ref[0, rows, cols]
            state = st_ref[h]
            inter = _dot_nt((qc * jnp.exp2(g2)).astype(BF16), state.astype(BF16))
            finish(rows, h, inter + intra_pairs(qc, g2, n2, vc))
            g_last = g2[c16 - 1:c16, :]
            k_dec = jnp.exp2(g_last + n2).astype(BF16)
            st_ref[h] = state * jnp.exp2(g_last) + _dot_tn(vc.astype(BF16), k_dec)
        return carry

    tril = (lax.broadcasted_iota(jnp.int32, (blk, blk), 0)
            >= lax.broadcasted_iota(jnp.int32, (blk, blk), 1))

    def block_factored(i, carry):
        rows = rows_of(i, blk)
        q_dec, k_inv, k_dec, v_bf, decay = [], [], [], [], []
        for h in range(hb):
            cols = slice(h * hd, (h + 1) * hd)
            off = ob_ref[rows, cols]
            g = gb_ref[rows, cols] + off
            n = hb_ref[rows, cols] - off
            g_last = g[blk - 1:blk, :]
            q_dec.append((qb_ref[rows, cols] * jnp.exp2(g)).astype(BF16))
            k_inv.append(jnp.exp2(n).astype(BF16))
            k_dec.append(jnp.exp2(g_last + n).astype(BF16))
            v_bf.append(v_ref[0, rows, cols].astype(BF16))
            decay.append(jnp.exp2(g_last))
        scores = [_dot_nt(q_dec[h], k_inv[h]) for h in range(hb)]
        inter = [_dot_nt(q_dec[h], st_ref[h].astype(BF16)) for h in range(hb)]
        for h in range(hb):
            st_ref[h] = st_ref[h] * decay[h] + _dot_tn(v_bf[h], k_dec[h])
        for h in range(hb):
            causal = jnp.where(tril, scores[h], 0.0).astype(BF16)
            finish(rows, h, inter[h] + _dot(causal, v_bf[h]))
        return carry

    mild = jnp.min(lowest) >= -HGRN_SAFE_BITS

    @pl.when(mild)
    def _():
        lax.fori_loop(0, tt // blk, block_factored, 0)

    @pl.when(jnp.logical_not(mild))
    def _():
        lax.fori_loop(0, tt // c16, step_pairs, 0)


def _hgrn_scan(proj3, lb, norm_gain, layer, *, heads, tt, hb):
    b, s, w4 = proj3.shape
    hd = HGRN_HEAD_DIM
    width = hb * hd
    ng = heads // hb

    def sec(k):
        return pl.BlockSpec((1, tt, width), lambda bi, hg, i: (bi, i, k * ng + hg))

    return pl.pallas_call(
        functools.partial(_hgrn_kernel, tt=tt, hb=hb),
        out_shape=jax.ShapeDtypeStruct((b, s, w4 // 4), BF16),
        grid=(b, ng, s // tt),
        in_specs=[sec(0), sec(1), sec(2), sec(3),
                  pl.BlockSpec((1, width), lambda bi, hg, i: (0, hg)),
                  _stacked((1, hd), layer, lambda bi, hg, i: (0, 0))],
        out_specs=pl.BlockSpec((1, tt, width), lambda bi, hg, i: (bi, i, hg)),
        scratch_shapes=[pltpu.VMEM((hb, hd, hd), F32), pltpu.VMEM((tt, width), F32),
                        pltpu.VMEM((tt, width), F32), pltpu.VMEM((tt, width), F32),
                        pltpu.VMEM((tt, width), F32)],
        compiler_params=_cparams(("parallel", "parallel", "arbitrary")),
        name="hgrn_scan",
    )(proj3, proj3, proj3, proj3, lb, norm_gain)


def _tiles(t, seq, hgrn_heads):
    big = min(1024, t)
    small = min(512, t)
    return {
        "ffn": dict(tm=big, tf=256),
        "ab_proj": dict(tm=big, tn=512),
        "norm_matmul": dict(tm=big, tn=1024),
        "pool": dict(ts=min(512, seq)),
        "attention": dict(tq=min(2048, seq), rs=min(512, seq)),
        "out_proj": dict(tm=small),
        "ple": dict(tm=small),
        "hgrn": dict(tt=min(256, seq), hb=hgrn_heads),
    }


def kernel(x, p, positions, ffn1_norm, ffn1_w_gate, ffn1_w_up, ffn1_w_down, mix_norm, ffn2_norm, ffn2_w_gate, ffn2_w_up, ffn2_w_down, ple_norm, ple_w_gate, ple_w_proj, ab_w_in, pool_w, pool_scale, diff_lambda, diff_subln, ab_w_out, hgrn_w_in, hgrn_lower_bounds, hgrn_norm, hgrn_w_out, final_norm):
    bsz, seq, d = x.shape
    depth = p.shape[0]
    t = bsz * seq
    tile = _tiles(t, seq, d // HGRN_HEAD_DIM)

    def gains(a):
        return a.reshape(a.shape[0], 1, a.shape[-1]).astype(F32)

    half = ROT_DIM // 2
    inv_freq = ROPE_THETA ** (-jnp.arange(0, ROT_DIM, 2, dtype=F32) / ROT_DIM)
    zeros = jnp.zeros((LANES - ROT_DIM,), F32)
    freq = jnp.concatenate([inv_freq, inv_freq, zeros]).reshape(1, LANES)
    sign = jnp.concatenate([-jnp.ones((half,), F32), jnp.ones((half,), F32), zeros]).reshape(1, LANES)
    pos = positions.reshape(t, 1).astype(F32)

    lbs = jax.nn.softmax(hgrn_lower_bounds.astype(F32), axis=0)
    lbs = jnp.cumsum(lbs, axis=0) - lbs[0]

    ffn1_g, ffn2_g, mix_g, ple_g = gains(ffn1_norm), gains(ffn2_norm), gains(mix_norm), gains(ple_norm)
    p3 = p.reshape(depth, t, p.shape[-1])
    xt = x.reshape(t, d)
    for i in range(depth):
        xt = _ffn(xt, ffn1_g, ffn1_w_gate, ffn1_w_up, ffn1_w_down, i, **tile["ffn"])
        if i % 2 == 0:
            e = i // 2
            lam_init = 0.8 - 0.6 * math.exp(-0.3 * i)
            u, qkv = _ab_proj(xt, mix_g, ab_w_in, pos, freq, sign, i, e, **tile["ab_proj"])
            pw = u.shape[1]
            a_out = _pool(u.reshape(bsz, seq, pw), pool_w, gains(pool_scale), e, **tile["pool"])
            heads = qkv.shape[1] // (3 * 2 * DIFF_HEAD_DIM)
            b_out = _diff_attention(qkv.reshape(bsz, seq, -1), diff_lambda.astype(F32),
                                    gains(diff_subln), lam_init, e, heads=heads,
                                    **tile["attention"])
            xt = _out_proj(xt, [a_out.reshape(t, -1), b_out.reshape(t, -1)], ab_w_out, e, **tile["out_proj"])
        else:
            o = i // 2
            proj = _norm_matmul(xt, mix_g, hgrn_w_in, i, o, **tile["norm_matmul"])
            heads = d // HGRN_HEAD_DIM
            mixed = _hgrn_scan(proj.reshape(bsz, seq, -1), lbs[i].reshape(1, -1), gains(hgrn_norm),
                               o, heads=heads, **tile["hgrn"])
            xt = _out_proj(xt, [mixed.reshape(t, -1)], hgrn_w_out, o, **tile["out_proj"])
        xt = _ffn(xt, ffn2_g, ffn2_w_gate, ffn2_w_up, ffn2_w_down, i, **tile["ffn"])
        xt = _ple(xt, p3, ple_g, ple_w_gate, ple_w_proj, final_norm.reshape(1, -1).astype(F32), i,
                  final_norm=(i == depth - 1), **tile["ple"])
    return xt.reshape(bsz, seq, d)
```

```python
import functools
import math

import jax
import jax.numpy as jnp
from jax import lax
from jax.experimental import pallas as pl
from jax.experimental.pallas import tpu as pltpu

F32 = jnp.float32
BF16 = jnp.bfloat16

NORM_EPS = 1e-6
SUBLN_EPS = 1e-5
POOL_WINDOWS = (2, 4, 8, 16)
POOL_HALO = 16
DIFF_HEAD_DIM = 128
ROT_DIM = DIFF_HEAD_DIM // 4
ROPE_THETA = 500000.0
HGRN_HEAD_DIM = 128
HGRN_CHUNK = 16
HGRN_BLOCK = 64
HGRN_SAFE_BITS = 100.0

LANES = 128
VMEM_LIMIT = 56 * 1024 * 1024


def _cparams(sem):
    return pltpu.CompilerParams(dimension_semantics=sem, vmem_limit_bytes=VMEM_LIMIT)


def _stacked(block, layer, index, **kwargs):
    return pl.BlockSpec((None,) + tuple(block), lambda *g: (layer,) + tuple(index(*g)), **kwargs)


def _rms(x, gain, eps):
    ms = jnp.mean(x * x, axis=-1, keepdims=True)
    return x * lax.rsqrt(ms + eps) * gain


def _silu(x):
    h = 0.5 * x
    return h + h * jnp.tanh(h)


def _dot(a, b):
    return jnp.dot(a, b, preferred_element_type=F32)


def _dot_nt(a, b):
    return lax.dot_general(a, b, (((1,), (1,)), ((), ())), preferred_element_type=F32)


def _dot_tn(a, b):
    return lax.dot_general(a, b, (((0,), (0,)), ((), ())), preferred_element_type=F32)


def _tree_sum(terms):
    while len(terms) > 1:
        terms = [terms[i] + terms[i + 1] for i in range(0, len(terms) - 1, 2)] + (
            [terms[-1]] if len(terms) % 2 else [])
    return terms[0]


def _ffn_kernel(x_ref, g_ref, wg_ref, wu_ref, wd_ref, o_ref, h_ref):
    j = pl.program_id(1)

    def down_proj():
        h = h_ref[...]
        a = _dot(h, wg_ref[...].astype(BF16))
        b = _dot(h, wu_ref[...].astype(BF16))
        act = (_silu(a) * b).astype(BF16)
        return _dot(act, wd_ref[...].astype(BF16))

    last = pl.num_programs(1) - 1

    @pl.when(j == 0)
    def _():
        h_ref[...] = _rms(x_ref[...], g_ref[...], NORM_EPS).astype(BF16)
        o_ref[...] = down_proj()

    @pl.when(jnp.logical_and(j > 0, j < last))
    def _():
        o_ref[...] += down_proj()

    @pl.when(j == last)
    def _():
        o_ref[...] = x_ref[...] + 0.5 * (o_ref[...] + down_proj())


def _ffn(x, gain, w_gate, w_up, w_down, layer, *, tm, tf):
    t, d = x.shape
    f = w_gate.shape[2]
    assert f // tf >= 2, "the first and the last hidden tile must be different grid steps"
    return pl.pallas_call(
        _ffn_kernel,
        out_shape=jax.ShapeDtypeStruct((t, d), F32),
        grid=(t // tm, f // tf),
        in_specs=[
            pl.BlockSpec((tm, d), lambda i, j: (i, 0)),
            _stacked((1, d), layer, lambda i, j: (0, 0)),
            _stacked((d, tf), layer, lambda i, j: (0, j)),
            _stacked((d, tf), layer, lambda i, j: (0, j)),
            _stacked((tf, d), layer, lambda i, j: (j, 0)),
        ],
        out_specs=pl.BlockSpec((tm, d), lambda i, j: (i, 0)),
        scratch_shapes=[pltpu.VMEM((tm, d), BF16)],
        compiler_params=_cparams(("parallel", "arbitrary")),
        name="ffn",
    )(x, gain, w_gate, w_up, w_down)


def _norm_matmul_kernel(x_ref, g_ref, w_ref, o_ref, h_ref):
    j = pl.program_id(1)

    def proj():
        return _dot(h_ref[...], w_ref[...].astype(BF16)).astype(o_ref.dtype)

    @pl.when(j == 0)
    def _():
        h_ref[...] = _rms(x_ref[...], g_ref[...], NORM_EPS).astype(BF16)
        o_ref[...] = proj()

    @pl.when(j > 0)
    def _():
        o_ref[...] = proj()


def _norm_matmul(x, gain, w, layer, w_layer, *, tm, tn):
    t, d = x.shape
    n = w.shape[2]
    return pl.pallas_call(
        _norm_matmul_kernel,
        out_shape=jax.ShapeDtypeStruct((t, n), F32),
        grid=(t // tm, n // tn),
        in_specs=[
            pl.BlockSpec((tm, d), lambda i, j: (i, 0)),
            _stacked((1, d), layer, lambda i, j: (0, 0)),
            _stacked((d, tn), w_layer, lambda i, j: (0, j)),
        ],
        out_specs=pl.BlockSpec((tm, tn), lambda i, j: (i, j)),
        scratch_shapes=[pltpu.VMEM((tm, d), BF16)],
        compiler_params=_cparams(("parallel", "arbitrary")),
        name="norm_matmul",
    )(x, gain, w)


def _rope(y, cos, sin_signed):
    lane = lax.broadcasted_iota(jnp.int32, cos.shape, 1)
    half = ROT_DIM // 2
    outs = []
    for c in range(y.shape[1] // LANES):
        t = y[:, c * LANES:(c + 1) * LANES]
        swapped = jnp.where(lane < half, pltpu.roll(t, LANES - half, axis=1),
                            pltpu.roll(t, half, axis=1))
        outs.append(t * cos + swapped * sin_signed)
    return jnp.concatenate(outs, axis=1)


def _ab_proj_kernel(x_ref, g_ref, w_ref, pos_ref, freq_ref, sign_ref, u_ref, qkv_ref,
                    h_ref, cos_ref, sin_ref, *, q_scale, per):
    j = pl.program_id(1)
    sec = j // per

    def proj():
        return _dot(h_ref[...], w_ref[...].astype(BF16))

    @pl.when(j == 0)
    def _():
        h_ref[...] = _rms(x_ref[...], g_ref[...], NORM_EPS).astype(BF16)
        ang = pos_ref[...] * freq_ref[...]
        cos_ref[...] = jnp.cos(ang)
        sin_ref[...] = jnp.sin(ang) * sign_ref[...]
        u_ref[...] = proj()

    @pl.when(jnp.logical_and(sec == 0, j > 0))
    def _():
        u_ref[...] = proj()

    @pl.when(sec == 1)
    def _():
        qkv_ref[...] = (_rope(proj(), cos_ref[...], sin_ref[...]) * q_scale).astype(BF16)

    @pl.when(sec == 2)
    def _():
        qkv_ref[...] = _rope(proj(), cos_ref[...], sin_ref[...]).astype(BF16)

    @pl.when(sec == 3)
    def _():
        qkv_ref[...] = proj().astype(BF16)


def _ab_proj(x, gain, w, pos, freq, sign, layer, w_layer, *, tm, tn):
    t, d = x.shape
    n = w.shape[2]
    per = n // 4 // tn
    q_scale = DIFF_HEAD_DIM ** -0.5 * math.log2(math.e)
    return pl.pallas_call(
        functools.partial(_ab_proj_kernel, q_scale=q_scale, per=per),
        out_shape=(jax.ShapeDtypeStruct((t, n // 4), F32),
                   jax.ShapeDtypeStruct((t, 3 * n // 4), BF16)),
        grid=(t // tm, 4 * per),
        in_specs=[
            pl.BlockSpec((tm, d), lambda i, j: (i, 0)),
            _stacked((1, d), layer, lambda i, j: (0, 0)),
            _stacked((d, tn), w_layer, lambda i, j: (0, j)),
            pl.BlockSpec((tm, 1), lambda i, j: (i, 0)),
            pl.BlockSpec((1, LANES), lambda i, j: (0, 0)),
            pl.BlockSpec((1, LANES), lambda i, j: (0, 0)),
        ],
        out_specs=(
            pl.BlockSpec((tm, tn), lambda i, j: (i, jnp.minimum(j, per - 1))),
            pl.BlockSpec((tm, tn), lambda i, j: (i, jnp.maximum(j - per, 0))),
        ),
        scratch_shapes=[pltpu.VMEM((tm, d), BF16), pltpu.VMEM((tm, LANES), F32),
                        pltpu.VMEM((tm, LANES), F32)],
        compiler_params=_cparams(("parallel", "arbitrary")),
        name="ab_proj",
    )(x, gain, w, pos, freq, sign)


def _pool_kernel(u_ref, w_ref, s_ref, o_ref, ext_ref, *, ts):
    i = pl.program_id(1)

    @pl.when(i == 0)
    def _():
        ext_ref[0:POOL_HALO, :] = jnp.zeros((POOL_HALO, ext_ref.shape[1]), F32)

    ext_ref[POOL_HALO:, :] = u_ref[0]
    gd = w_ref.shape[1]
    row = lax.broadcasted_iota(jnp.int32, (ts, gd), 0) + i * ts
    for g, win in enumerate(POOL_WINDOWS):
        cols = slice(g * gd, (g + 1) * gd)
        acc = ext_ref[:, cols]
        shift = 1
        while shift < win:
            acc = acc + pltpu.roll(acc, shift, axis=0)
            shift *= 2
        cnt = jnp.minimum(row + 1, win).astype(F32)
        v = u_ref[0, :, cols]
        dlt = acc[POOL_HALO:, :] / cnt - v
        y = _dot(dlt.astype(BF16), w_ref[g].astype(BF16)) * s_ref[:, cols]
        o_ref[0, :, cols] = y.astype(o_ref.dtype)
    ext_ref[0:POOL_HALO, :] = ext_ref[ts:ts + POOL_HALO, :]


def _pool(u3, pool_w, pool_scale, layer, *, ts):
    b, s, pw = u3.shape
    _, g, gd, _ = pool_w.shape
    return pl.pallas_call(
        functools.partial(_pool_kernel, ts=ts),
        out_shape=jax.ShapeDtypeStruct((b, s, pw), BF16),
        grid=(b, s // ts),
        in_specs=[
            pl.BlockSpec((1, ts, pw), lambda bi, i: (bi, i, 0)),
            _stacked((g, gd, gd), layer, lambda bi, i: (0, 0, 0)),
            _stacked((1, pw), layer, lambda bi, i: (0, 0)),
        ],
        out_specs=pl.BlockSpec((1, ts, pw), lambda bi, i: (bi, i, 0)),
        scratch_shapes=[pltpu.VMEM((ts + POOL_HALO, pw), F32)],
        compiler_params=_cparams(("parallel", "arbitrary")),
        name="pool",
    )(u3, pool_w, pool_scale)


def _attn_kernel(lam_ref, q_ref, k_ref, v_ref, sg_ref, o_ref, m_ref, l_ref, acc_ref, s_ref,
                 p_ref, a_ref, *, tq, rs, lam_init):
    dh = DIFF_HEAD_DIM
    strip = 16
    qi = pl.program_id(2)
    m_ref[...] = jnp.full(m_ref.shape, -jnp.inf, F32)
    l_ref[...] = jnp.zeros(l_ref.shape, F32)
    acc_ref[...] = jnp.zeros(acc_ref.shape, F32)

    def step(kb, masked):
        r0 = pl.multiple_of(kb * tq, tq)

        def width(r):
            return (r + 1) * rs if masked else tq

        def qk(r):
            nk = width(r)
            for c in range(2):
                comp = slice(c * dh, (c + 1) * dh)
                s_ref[r % 2, c, :, :nk] = _dot_nt(q_ref[0, r * rs:(r + 1) * rs, comp],
                                                  k_ref[0, pl.ds(r0, nk), comp])

        def softmax(r):
            nk = width(r)
            for c in range(2):
                for i in range(rs // strip):
                    lo = r * rs + i * strip
                    rows = slice(lo, lo + strip)
                    ns = min(nk, -(-(lo + strip) // LANES) * LANES) if masked else nk
                    s = s_ref[r % 2, c, i * strip:(i + 1) * strip, :ns]
                    if masked:
                        row = lax.broadcasted_iota(jnp.int32, s.shape, 0) + lo
                        col = lax.broadcasted_iota(jnp.int32, s.shape, 1)
                        s = jnp.where(col <= row, s, -jnp.inf)
                    m_prev = m_ref[c, rows]
                    m_new = jnp.maximum(m_prev, jnp.max(s, axis=-1, keepdims=True))
                    alpha = jnp.exp2(m_prev - m_new)
                    p = jnp.exp2(s - m_new)
                    part = _tree_sum([p[:, j * LANES:(j + 1) * LANES] for j in range(ns // LANES)])
                    l_ref[c, rows] = alpha * l_ref[c, rows] + part
                    m_ref[c, rows] = m_new
                    a_ref[r % 2, c, i * strip:(i + 1) * strip] = alpha
                    prow = slice(c * rs + i * strip, c * rs + (i + 1) * strip)
                    p_ref[r % 2, prow, :ns] = p.astype(BF16)
                    if ns < nk:
                        p_ref[r % 2, prow, ns:nk] = jnp.zeros((strip, nk - ns), BF16)

        def pv(r):
            nk = width(r)
            out = _dot(p_ref[r % 2, :, :nk], v_ref[0, pl.ds(r0, nk), :])
            rows = slice(r * rs, (r + 1) * rs)
            acc_ref[0, rows] = a_ref[r % 2, 0] * acc_ref[0, rows] + out[:rs]
            acc_ref[1, rows] = a_ref[r % 2, 1] * acc_ref[1, rows] + out[rs:]

        nr = tq // rs
        qk(0)
        for r in range(nr):
            if r + 1 < nr:
                qk(r + 1)
            softmax(r)
            pv(r)

    def body(kb, carry):
        step(kb, False)
        return carry

    lax.fori_loop(0, qi, body, 0)
    step(qi, True)

    lp = lam_ref[...]
    lam = (jnp.exp(jnp.sum(lp[0:1] * lp[1:2], axis=-1, keepdims=True))
           - jnp.exp(jnp.sum(lp[2:3] * lp[3:4], axis=-1, keepdims=True)) + lam_init)
    l0 = jnp.sum(l_ref[0], axis=-1, keepdims=True)
    l1 = jnp.sum(l_ref[1], axis=-1, keepdims=True)
    o = acc_ref[0] / l0 - lam * (acc_ref[1] / l1)
    o = _rms(o, sg_ref[...], SUBLN_EPS) * (1.0 - lam_init)
    o_ref[0] = o.astype(o_ref.dtype)


def _diff_attention(qkv3, lam_params, subln, lam_init, layer, *, heads, tq, rs):
    b, s, w3 = qkv3.shape
    hw = 2 * DIFF_HEAD_DIM
    return pl.pallas_call(
        functools.partial(_attn_kernel, tq=tq, rs=rs, lam_init=lam_init),
        out_shape=jax.ShapeDtypeStruct((b, s, w3 // 3), BF16),
        grid=(b, heads, s // tq),
        in_specs=[
            _stacked((4, DIFF_HEAD_DIM), layer, lambda bi, h, i: (0, 0)),
            pl.BlockSpec((1, tq, hw), lambda bi, h, i: (bi, i, h)),
            pl.BlockSpec((1, s, hw), lambda bi, h, i: (bi, 0, heads + h),
                         pipeline_mode=pl.Buffered(1)),
            pl.BlockSpec((1, s, hw), lambda bi, h, i: (bi, 0, 2 * heads + h),
                         pipeline_mode=pl.Buffered(1)),
            _stacked((1, hw), layer, lambda bi, h, i: (0, 0)),
        ],
        out_specs=pl.BlockSpec((1, tq, hw), lambda bi, h, i: (bi, i, h)),
        scratch_shapes=[pltpu.VMEM((2, tq, 1), F32), pltpu.VMEM((2, tq, LANES), F32),
                        pltpu.VMEM((2, tq, hw), F32), pltpu.VMEM((2, 2, rs, tq), F32),
                        pltpu.VMEM((2, 2 * rs, tq), BF16), pltpu.VMEM((2, 2, rs, 1), F32)],
        compiler_params=_cparams(("parallel", "parallel", "arbitrary")),
        name="diff_attention",
    )(lam_params, qkv3, qkv3, qkv3, subln)


def _out_proj_kernel(*refs):
    x_ref, w_ref, o_ref, wb_ref = refs[0], refs[-3], refs[-2], refs[-1]

    @pl.when(pl.program_id(0) == 0)
    def _():
        wb_ref[...] = w_ref[...].astype(BF16)

    acc = x_ref[...]
    k0 = 0
    for a_ref in refs[1:-3]:
        kw = a_ref.shape[1]
        acc = acc + _dot(a_ref[...], wb_ref[k0:k0 + kw, :])
        k0 += kw
    o_ref[...] = acc


def _out_proj(x, parts, w, layer, *, tm):
    t, d = x.shape
    k = w.shape[1]
    in_specs = [pl.BlockSpec((tm, d), lambda i: (i, 0))]
    in_specs += [pl.BlockSpec((tm, a.shape[1]), lambda i: (i, 0)) for a in parts]
    in_specs += [_stacked((k, d), layer, lambda i: (0, 0), pipeline_mode=pl.Buffered(1))]
    return pl.pallas_call(
        _out_proj_kernel,
        out_shape=jax.ShapeDtypeStruct((t, d), F32),
        grid=(t // tm,),
        in_specs=in_specs,
        out_specs=pl.BlockSpec((tm, d), lambda i: (i, 0)),
        scratch_shapes=[pltpu.VMEM((k, d), BF16)],
        compiler_params=_cparams(("arbitrary",)),
        name="out_proj",
    )(x, *parts, w)


def _ple_kernel(x_ref, p_ref, g_ref, wg_ref, wp_ref, fg_ref, o_ref, wgb_ref, wpb_ref, *,
                final_norm):
    @pl.when(pl.program_id(0) == 0)
    def _():
        wgb_ref[...] = wg_ref[...].astype(BF16)
        wpb_ref[...] = wp_ref[...].astype(BF16)

    x = x_ref[...]
    h = _rms(x, g_ref[...], NORM_EPS).astype(BF16)
    gate = jax.nn.sigmoid(_dot(h, wgb_ref[...]))
    y = x + _dot(p_ref[...].astype(BF16), wpb_ref[...]) * gate
    if final_norm:
        y = _rms(y, fg_ref[...], NORM_EPS)
    o_ref[...] = y


def _ple(x, p, gain, w_gate, w_proj, final_gain, layer, *, tm, final_norm):
    t, d = x.shape
    pd = p.shape[2]
    once = pl.Buffered(1)
    return pl.pallas_call(
        functools.partial(_ple_kernel, final_norm=final_norm),
        out_shape=jax.ShapeDtypeStruct((t, d), F32),
        grid=(t // tm,),
        in_specs=[
            pl.BlockSpec((tm, d), lambda i: (i, 0)),
            _stacked((tm, pd), layer, lambda i: (i, 0)),
            _stacked((1, d), layer, lambda i: (0, 0)),
            _stacked((d, d), layer, lambda i: (0, 0), pipeline_mode=once),
            _stacked((pd, d), layer, lambda i: (0, 0), pipeline_mode=once),
            pl.BlockSpec((1, d), lambda i: (0, 0)),
        ],
        out_specs=pl.BlockSpec((tm, d), lambda i: (i, 0)),
        scratch_shapes=[pltpu.VMEM((d, d), BF16), pltpu.VMEM((pd, d), BF16)],
        compiler_params=_cparams(("arbitrary",)),
        name="ple",
    )(x, p, gain, w_gate, w_proj, final_gain)


def _hgrn_kernel(q_ref, f_ref, v_ref, g_ref, lb_ref, ng_ref, o_ref, st_ref, qb_ref, gb_ref,
                 hb_ref, ob_ref, *, tt, hb):
    c16 = HGRN_CHUNK
    blk = HGRN_BLOCK
    per = blk // c16
    half = c16 // 2
    hd = HGRN_HEAD_DIM
    width = hb * hd
    log2e = math.log2(math.e)

    @pl.when(pl.program_id(2) == 0)
    def _():
        st_ref[...] = jnp.zeros(st_ref.shape, F32)

    lb = lb_ref[...]
    log_lb = jnp.log(lb)
    log_1m_lb = jnp.log1p(-lb)
    row = lax.broadcasted_iota(jnp.int32, (c16, width), 0)

    def rows_of(c, n=c16):
        return pl.ds(pl.multiple_of(c * n, n), n)

    def gates(i, lowest):
        off = jnp.zeros((1, width), F32)
        for j in range(per):
            rows = rows_of(i * per + j)
            q = q_ref[0, rows, :]
            qb_ref[rows, :] = _silu(q)
            ff = f_ref[0, rows, :]
            log_sig = jnp.minimum(ff, 0.0) - jnp.log(1.0 + jnp.exp(-jnp.abs(ff)))
            b = log_1m_lb + log_sig
            log_f = jnp.maximum(log_lb, b) + jnp.log(1.0 + jnp.exp(-jnp.abs(log_lb - b)))
            log_k = b - ff
            cum = log_f
            shift = 1
            while shift < c16:
                cum = cum + jnp.where(row >= shift, pltpu.roll(cum, shift, axis=0), 0.0)
                shift *= 2
            g2 = cum * log2e
            gb_ref[rows, :] = g2
            hb_ref[rows, :] = (log_k - cum) * log2e
            ob_ref[rows, :] = jnp.broadcast_to(off, (c16, width))
            off = off + g2[c16 - 1:c16, :]
        return jnp.minimum(lowest, off)

    lowest = lax.fori_loop(0, tt // blk, gates, jnp.zeros((1, width), F32))

    def finish(rows, h, out):
        cols = slice(h * hd, (h + 1) * hd)
        gate = g_ref[0, rows, cols]
        y = _rms(out, ng_ref[...], NORM_EPS) * _silu(gate)
        o_ref[0, rows, cols] = y.astype(o_ref.dtype)

    ones = jnp.ones((hd, hd), BF16)
    t8 = lax.broadcasted_iota(jnp.int32, (half, hd), 0)

    def intra_pairs(qc, g2, n2, vc):
        g_top, g_bot = g2[:half], g2[half:]
        q_top, q_bot = qc[:half], qc[half:]
        lhs = []
        for s in range(half):
            ns = n2[s:s + 1, :]
            d_top = g_top + ns
            if s > 0:
                d_top = jnp.where(t8 >= s, d_top, -jnp.inf)
            p = jnp.concatenate([q_top * jnp.exp2(d_top), q_bot * jnp.exp2(g_bot + ns)], axis=0)
            lhs.append(p.astype(BF16))
        bots = []
        for s in range(half, c16):
            d_bot = g_bot + n2[s:s + 1, :]
            if s > half:
                d_bot = jnp.where(t8 >= s - half, d_bot, -jnp.inf)
            bots.append(q_bot * jnp.exp2(d_bot))
        for j in range(0, half, 2):
            lhs.append(jnp.concatenate([bots[j], bots[j + 1]], axis=0).astype(BF16))
        red = _dot(jnp.concatenate(lhs, axis=0), ones)
        top_terms = [red[s * c16:s * c16 + half] * vc[s:s + 1, :] for s in range(half)]
        bot_terms = [red[s * c16 + half:(s + 1) * c16] * vc[s:s + 1, :] for s in range(half)]
        base = half * c16
        bot_terms += [red[base + j * half:base + (j + 1) * half] * vc[half + j:half + j + 1, :]
                      for j in range(half)]
        return jnp.concatenate([_tree_sum(top_terms), _tree_sum(bot_terms)], axis=0)

    def step_pairs(c, carry):
        rows = rows_of(c)
        for h in range(hb):
            cols = slice(h * hd, (h + 1) * hd)
            g2, n2, qc, vc = gb_ref[rows, cols], hb_ref[rows, cols], qb_ref[rows, cols], v_ref[0, rows, cols]
            state = st_ref[h]
            inter = _dot_nt((qc * jnp.exp2(g2)).astype(BF16), state.astype(BF16))
            finish(rows, h, inter + intra_pairs(qc, g2, n2, vc))
            g_last = g2[c16 - 1:c16, :]
            k_dec = jnp.exp2(g_last + n2).astype(BF16)
            st_ref[h] = state * jnp.exp2(g_last) + _dot_tn(vc.astype(BF16), k_dec)
        return carry

    tril = (lax.broadcasted_iota(jnp.int32, (blk, blk), 0)
            >= lax.broadcasted_iota(jnp.int32, (blk, blk), 1))

    def block_factored(i, carry):
        rows = rows_of(i, blk)
        q_dec, k_inv, k_dec, v_bf, decay = [], [], [], [], []
        for h in range(hb):
            cols = slice(h * hd, (h + 1) * hd)
            off = ob_ref[rows, cols]
            g = gb_ref[rows, cols] + off
            n = hb_ref[rows, cols] - off
            g_last = g[blk - 1:blk, :]
            q_dec.append((qb_ref[rows, cols] * jnp.exp2(g)).astype(BF16))
            k_inv.append(jnp.exp2(n).astype(BF16))
            k_dec.append(jnp.exp2(g_last + n).astype(BF16))
            v_bf.append(v_ref[0, rows, cols].astype(BF16))
            decay.append(jnp.exp2(g_last))
        scores = [_dot_nt(q_dec[h], k_inv[h]) for h in range(hb)]
        inter = [_dot_nt(q_dec[h], st_ref[h].astype(BF16)) for h in range(hb)]
        for h in range(hb):
            st_ref[h] = st_ref[h] * decay[h] + _dot_tn(v_bf[h], k_dec[h])
        for h in range(hb):
            causal = jnp.where(tril, scores[h], 0.0).astype(BF16)
            finish(rows, h, inter[h] + _dot(causal, v_bf[h]))
        return carry

    mild = jnp.min(lowest) >= -HGRN_SAFE_BITS

    @pl.when(mild)
    def _():
        lax.fori_loop(0, tt // blk, block_factored, 0)

    @pl.when(jnp.logical_not(mild))
    def _():
        lax.fori_loop(0, tt // c16, step_pairs, 0)


def _hgrn_scan(proj3, lb, norm_gain, layer, *, heads, tt, hb):
    b, s, w4 = proj3.shape
    hd = HGRN_HEAD_DIM
    width = hb * hd
    ng = heads // hb

    def sec(k):
        return pl.BlockSpec((1, tt, width), lambda bi, hg, i: (bi, i, k * ng + hg))

    return pl.pallas_call(
        functools.partial(_hgrn_kernel, tt=tt, hb=hb),
        out_shape=jax.ShapeDtypeStruct((b, s, w4 // 4), BF16),
        grid=(b, ng, s // tt),
        in_specs=[sec(0), sec(1), sec(2), sec(3),
                  pl.BlockSpec((1, width), lambda bi, hg, i: (0, hg)),
                  _stacked((1, hd), layer, lambda bi, hg, i: (0, 0))],
        out_specs=pl.BlockSpec((1, tt, width), lambda bi, hg, i: (bi, i, hg)),
        scratch_shapes=[pltpu.VMEM((hb, hd, hd), F32), pltpu.VMEM((tt, width), F32),
                        pltpu.VMEM((tt, width), F32), pltpu.VMEM((tt, width), F32),
                        pltpu.VMEM((tt, width), F32)],
        compiler_params=_cparams(("parallel", "parallel", "arbitrary")),
        name="hgrn_scan",
    )(proj3, proj3, proj3, proj3, lb, norm_gain)


def _tiles(t, seq, hgrn_heads):
    big = min(1024, t)
    small = min(512, t)
    return {
        "ffn": dict(tm=big, tf=256),
        "ab_proj": dict(tm=big, tn=512),
        "norm_matmul": dict(tm=big, tn=1024),
        "pool": dict(ts=min(512, seq)),
        "attention": dict(tq=min(2048, seq), rs=min(512, seq)),
        "out_proj": dict(tm=small),
        "ple": dict(tm=small),
        "hgrn": dict(tt=min(256, seq), hb=hgrn_heads),
    }


def kernel(x, p, positions, ffn1_norm, ffn1_w_gate, ffn1_w_up, ffn1_w_down, mix_norm, ffn2_norm, ffn2_w_gate, ffn2_w_up, ffn2_w_down, ple_norm, ple_w_gate, ple_w_proj, ab_w_in, pool_w, pool_scale, diff_lambda, diff_subln, ab_w_out, hgrn_w_in, hgrn_lower_bounds, hgrn_norm, hgrn_w_out, final_norm):
    bsz, seq, d = x.shape
    depth = p.shape[0]
    t = bsz * seq
    tile = _tiles(t, seq, d // HGRN_HEAD_DIM)

    def gains(a):
        return a.reshape(a.shape[0], 1, a.shape[-1]).astype(F32)

    half = ROT_DIM // 2
    inv_freq = ROPE_THETA ** (-jnp.arange(0, ROT_DIM, 2, dtype=F32) / ROT_DIM)
    zeros = jnp.zeros((LANES - ROT_DIM,), F32)
    freq = jnp.concatenate([inv_freq, inv_freq, zeros]).reshape(1, LANES)
    sign = jnp.concatenate([-jnp.ones((half,), F32), jnp.ones((half,), F32), zeros]).reshape(1, LANES)
    pos = positions.reshape(t, 1).astype(F32)

    lbs = jax.nn.softmax(hgrn_lower_bounds.astype(F32), axis=0)
    lbs = jnp.cumsum(lbs, axis=0) - lbs[0]

    ffn1_g, ffn2_g, mix_g, ple_g = gains(ffn1_norm), gains(ffn2_norm), gains(mix_norm), gains(ple_norm)
    p3 = p.reshape(depth, t, p.shape[-1])
    xt = x.reshape(t, d)
    for i in range(depth):
        xt = _ffn(xt, ffn1_g, ffn1_w_gate, ffn1_w_up, ffn1_w_down, i, **tile["ffn"])
        if i % 2 == 0:
            e = i // 2
            lam_init = 0.8 - 0.6 * math.exp(-0.3 * i)
            u, qkv = _ab_proj(xt, mix_g, ab_w_in, pos, freq, sign, i, e, **tile["ab_proj"])
            pw = u.shape[1]
            a_out = _pool(u.reshape(bsz, seq, pw), pool_w, gains(pool_scale), e, **tile["pool"])
            heads = qkv.shape[1] // (3 * 2 * DIFF_HEAD_DIM)
            b_out = _diff_attention(qkv.reshape(bsz, seq, -1), diff_lambda.astype(F32),
                                    gains(diff_subln), lam_init, e, heads=heads,
                                    **tile["attention"])
            xt = _out_proj(xt, [a_out.reshape(t, -1), b_out.reshape(t, -1)], ab_w_out, e, **tile["out_proj"])
        else:
            o = i // 2
            proj = _norm_matmul(xt, mix_g, hgrn_w_in, i, o, **tile["norm_matmul"])
            heads = d // HGRN_HEAD_DIM
            mixed = _hgrn_scan(proj.reshape(bsz, seq, -1), lbs[i].reshape(1, -1), gains(hgrn_norm),
                               o, heads=heads, **tile["hgrn"])
            xt = _out_proj(xt, [mixed.reshape(t, -1)], hgrn_w_out, o, **tile["out_proj"])
        xt = _ffn(xt, ffn2_g, ffn2_w_gate, ffn2_w_up, ffn2_w_down, i, **tile["ffn"])
        xt = _ple(xt, p3, ple_g, ple_w_gate, ple_w_proj, final_norm.reshape(1, -1).astype(F32), i,
                  final_norm=(i == depth - 1), **tile["ple"])
    return xt.reshape(bsz, seq, d)
```

```python
import functools
import math

import jax
import jax.numpy as jnp
from jax import lax
from jax.experimental import pallas as pl
from jax.experimental.pallas import tpu as pltpu

F32 = jnp.float32
BF16 = jnp.bfloat16

NORM_EPS = 1e-6
SUBLN_EPS = 1e-5
POOL_WINDOWS = (2, 4, 8, 16)
POOL_HALO = 16
DIFF_HEAD_DIM = 128
ROT_DIM = DIFF_HEAD_DIM // 4
ROPE_THETA = 500000.0
HGRN_HEAD_DIM = 128
HGRN_CHUNK = 16
HGRN_BLOCK = 64
HGRN_SAFE_BITS = 100.0

LANES = 128
VMEM_LIMIT = 56 * 1024 * 1024


def _cparams(sem):
    return pltpu.CompilerParams(dimension_semantics=sem, vmem_limit_bytes=VMEM_LIMIT)


def _stacked(block, layer, index, **kwargs):
    return pl.BlockSpec((None,) + tuple(block), lambda *g: (layer,) + tuple(index(*g)), **kwargs)


def _rms(x, gain, eps):
    ms = jnp.mean(x * x, axis=-1, keepdims=True)
    return x * lax.rsqrt(ms + eps) * gain


def _silu(x):
    h = 0.5 * x
    return h + h * jnp.tanh(h)


def _dot(a, b):
    return jnp.dot(a, b, preferred_element_type=F32)


def _dot_nt(a, b):
    return lax.dot_general(a, b, (((1,), (1,)), ((), ())), preferred_element_type=F32)


def _dot_tn(a, b):
    return lax.dot_general(a, b, (((0,), (0,)), ((), ())), preferred_element_type=F32)


def _tree_sum(terms):
    while len(terms) > 1:
        terms = [terms[i] + terms[i + 1] for i in range(0, len(terms) - 1, 2)] + (
            [terms[-1]] if len(terms) % 2 else [])
    return terms[0]


def _ffn_kernel(x_ref, g_ref, wg_ref, wu_ref, wd_ref, o_ref, h_ref):
    j = pl.program_id(1)

    def down_proj():
        h = h_ref[...]
        a = _dot(h, wg_ref[...].astype(BF16))
        b = _dot(h, wu_ref[...].astype(BF16))
        act = (_silu(a) * b).astype(BF16)
        return _dot(act, wd_ref[...].astype(BF16))

    last = pl.num_programs(1) - 1

    @pl.when(j == 0)
    def _():
        h_ref[...] = _rms(x_ref[...], g_ref[...], NORM_EPS).astype(BF16)
        o_ref[...] = down_proj()

    @pl.when(jnp.logical_and(j > 0, j < last))
    def _():
        o_ref[...] += down_proj()

    @pl.when(j == last)
    def _():
        o_ref[...] = x_ref[...] + 0.5 * (o_ref[...] + down_proj())


def _ffn(x, gain, w_gate, w_up, w_down, layer, *, tm, tf):
    t, d = x.shape
    f = w_gate.shape[2]
    assert f // tf >= 2, "the first and the last hidden tile must be different grid steps"
    return pl.pallas_call(
        _ffn_kernel,
        out_shape=jax.ShapeDtypeStruct((t, d), F32),
        grid=(t // tm, f // tf),
        in_specs=[
            pl.BlockSpec((tm, d), lambda i, j: (i, 0)),
            _stacked((1, d), layer, lambda i, j: (0, 0)),
            _stacked((d, tf), layer, lambda i, j: (0, j)),
            _stacked((d, tf), layer, lambda i, j: (0, j)),
            _stacked((tf, d), layer, lambda i, j: (j, 0)),
        ],
        out_specs=pl.BlockSpec((tm, d), lambda i, j: (i, 0)),
        scratch_shapes=[pltpu.VMEM((tm, d), BF16)],
        compiler_params=_cparams(("parallel", "arbitrary")),
        name="ffn",
    )(x, gain, w_gate, w_up, w_down)


def _norm_matmul_kernel(x_ref, g_ref, w_ref, o_ref, h_ref):
    j = pl.program_id(1)

    def proj():
        return _dot(h_ref[...], w_ref[...].astype(BF16)).astype(o_ref.dtype)

    @pl.when(j == 0)
    def _():
        h_ref[...] = _rms(x_ref[...], g_ref[...], NORM_EPS).astype(BF16)
        o_ref[...] = proj()

    @pl.when(j > 0)
    def _():
        o_ref[...] = proj()


def _norm_matmul(x, gain, w, layer, w_layer, *, tm, tn):
    t, d = x.shape
    n = w.shape[2]
    return pl.pallas_call(
        _norm_matmul_kernel,
        out_shape=jax.ShapeDtypeStruct((t, n), F32),
        grid=(t // tm, n // tn),
        in_specs=[
            pl.BlockSpec((tm, d), lambda i, j: (i, 0), pipeline_mode=pl.Buffered(1)),
            _stacked((1, d), layer, lambda i, j: (0, 0)),
            _stacked((d, tn), w_layer, lambda i, j: (0, j)),
        ],
        out_specs=pl.BlockSpec((tm, tn), lambda i, j: (i, j)),
        scratch_shapes=[pltpu.VMEM((tm, d), BF16)],
        compiler_params=_cparams(("parallel", "arbitrary")),
        name="norm_matmul",
    )(x, gain, w)


def _rope(y, cos, sin_signed):
    lane = lax.broadcasted_iota(jnp.int32, cos.shape, 1)
    half = ROT_DIM // 2
    outs = []
    for c in range(y.shape[1] // LANES):
        t = y[:, c * LANES:(c + 1) * LANES]
        swapped = jnp.where(lane < half, pltpu.roll(t, LANES - half, axis=1),
                            pltpu.roll(t, half, axis=1))
        outs.append(t * cos + swapped * sin_signed)
    return jnp.concatenate(outs, axis=1)


def _ab_proj_kernel(x_ref, g_ref, w_ref, pos_ref, freq_ref, sign_ref, u_ref, qkv_ref,
                    h_ref, cos_ref, sin_ref, *, q_scale, per):
    j = pl.program_id(1)
    sec = j // per

    def proj():
        return _dot(h_ref[...], w_ref[...].astype(BF16))

    @pl.when(j == 0)
    def _():
        h_ref[...] = _rms(x_ref[...], g_ref[...], NORM_EPS).astype(BF16)
        ang = pos_ref[...] * freq_ref[...]
        cos_ref[...] = jnp.cos(ang)
        sin_ref[...] = jnp.sin(ang) * sign_ref[...]
        u_ref[...] = proj()

    @pl.when(jnp.logical_and(sec == 0, j > 0))
    def _():
        u_ref[...] = proj()

    @pl.when(sec == 1)
    def _():
        qkv_ref[...] = (_rope(proj(), cos_ref[...], sin_ref[...]) * q_scale).astype(BF16)

    @pl.when(sec == 2)
    def _():
        qkv_ref[...] = _rope(proj(), cos_ref[...], sin_ref[...]).astype(BF16)

    @pl.when(sec == 3)
    def _():
        qkv_ref[...] = proj().astype(BF16)


def _ab_proj(x, gain, w, pos, freq, sign, layer, w_layer, *, tm, tn):
    t, d = x.shape
    n = w.shape[2]
    per = n // 4 // tn
    q_scale = DIFF_HEAD_DIM ** -0.5 * math.log2(math.e)
    return pl.pallas_call(
        functools.partial(_ab_proj_kernel, q_scale=q_scale, per=per),
        out_shape=(jax.ShapeDtypeStruct((t, n // 4), F32),
                   jax.ShapeDtypeStruct((t, 3 * n // 4), BF16)),
        grid=(t // tm, 4 * per),
        in_specs=[
            pl.BlockSpec((tm, d), lambda i, j: (i, 0)),
            _stacked((1, d), layer, lambda i, j: (0, 0)),
            _stacked((d, tn), w_layer, lambda i, j: (0, j)),
            pl.BlockSpec((tm, 1), lambda i, j: (i, 0)),
            pl.BlockSpec((1, LANES), lambda i, j: (0, 0)),
            pl.BlockSpec((1, LANES), lambda i, j: (0, 0)),
        ],
        out_specs=(
            pl.BlockSpec((tm, tn), lambda i, j: (i, jnp.minimum(j, per - 1))),
            pl.BlockSpec((tm, tn), lambda i, j: (i, jnp.maximum(j - per, 0))),
        ),
        scratch_shapes=[pltpu.VMEM((tm, d), BF16), pltpu.VMEM((tm, LANES), F32),
                        pltpu.VMEM((tm, LANES), F32)],
        compiler_params=_cparams(("parallel", "arbitrary")),
        name="ab_proj",
    )(x, gain, w, pos, freq, sign)


def _pool_kernel(u_ref, w_ref, s_ref, o_ref, ext_ref, *, ts):
    i = pl.program_id(1)

    @pl.when(i == 0)
    def _():
        ext_ref[0:POOL_HALO, :] = jnp.zeros((POOL_HALO, ext_ref.shape[1]), F32)

    ext_ref[POOL_HALO:, :] = u_ref[0]
    gd = w_ref.shape[1]
    row = lax.broadcasted_iota(jnp.int32, (ts, gd), 0) + i * ts
    for g, win in enumerate(POOL_WINDOWS):
        cols = slice(g * gd, (g + 1) * gd)
        acc = ext_ref[:, cols]
        shift = 1
        while shift < win:
            acc = acc + pltpu.roll(acc, shift, axis=0)
            shift *= 2
        cnt = jnp.minimum(row + 1, win).astype(F32)
        v = u_ref[0, :, cols]
        dlt = acc[POOL_HALO:, :] / cnt - v
        y = _dot(dlt.astype(BF16), w_ref[g].astype(BF16)) * s_ref[:, cols]
        o_ref[0, :, cols] = y.astype(o_ref.dtype)
    ext_ref[0:POOL_HALO, :] = ext_ref[ts:ts + POOL_HALO, :]


def _pool(u3, pool_w, pool_scale, layer, *, ts):
    b, s, pw = u3.shape
    _, g, gd, _ = pool_w.shape
    return pl.pallas_call(
        functools.partial(_pool_kernel, ts=ts),
        out_shape=jax.ShapeDtypeStruct((b, s, pw), BF16),
        grid=(b, s // ts),
        in_specs=[
            pl.BlockSpec((1, ts, pw), lambda bi, i: (bi, i, 0)),
            _stacked((g, gd, gd), layer, lambda bi, i: (0, 0, 0)),
            _stacked((1, pw), layer, lambda bi, i: (0, 0)),
        ],
        out_specs=pl.BlockSpec((1, ts, pw), lambda bi, i: (bi, i, 0)),
        scratch_shapes=[pltpu.VMEM((ts + POOL_HALO, pw), F32)],
        compiler_params=_cparams(("parallel", "arbitrary")),
        name="pool",
    )(u3, pool_w, pool_scale)


def _attn_kernel(lam_ref, q_ref, k_ref, v_ref, sg_ref, o_ref, m_ref, l_ref, acc_ref, s_ref,
                 p_ref, a_ref, *, tq, rs, lam_init):
    dh = DIFF_HEAD_DIM
    strip = 16
    qi = pl.program_id(2)
    m_ref[...] = jnp.full(m_ref.shape, -jnp.inf, F32)
    l_ref[...] = jnp.zeros(l_ref.shape, F32)
    acc_ref[...] = jnp.zeros(acc_ref.shape, F32)

    def step(kb, masked):
        r0 = pl.multiple_of(kb * tq, tq)

        def width(r):
            return (r + 1) * rs if masked else tq

        def qk(r):
            nk = width(r)
            for c in range(2):
                comp = slice(c * dh, (c + 1) * dh)
                s_ref[r % 2, c, :, :nk] = _dot_nt(q_ref[0, r * rs:(r + 1) * rs, comp],
                                                  k_ref[0, pl.ds(r0, nk), comp])

        def softmax(r):
            nk = width(r)
            for c in range(2):
                for i in range(rs // strip):
                    lo = r * rs + i * strip
                    rows = slice(lo, lo + strip)
                    ns = min(nk, -(-(lo + strip) // LANES) * LANES) if masked else nk
                    s = s_ref[r % 2, c, i * strip:(i + 1) * strip, :ns]
                    if masked:
                        row = lax.broadcasted_iota(jnp.int32, s.shape, 0) + lo
                        col = lax.broadcasted_iota(jnp.int32, s.shape, 1)
                        s = jnp.where(col <= row, s, -jnp.inf)
                    m_prev = m_ref[c, rows]
                    m_new = jnp.maximum(m_prev, jnp.max(s, axis=-1, keepdims=True))
                    alpha = jnp.exp2(m_prev - m_new)
                    p = jnp.exp2(s - m_new)
                    part = _tree_sum([p[:, j * LANES:(j + 1) * LANES] for j in range(ns // LANES)])
                    l_ref[c, rows] = alpha * l_ref[c, rows] + part
                    m_ref[c, rows] = m_new
                    a_ref[r % 2, c, i * strip:(i + 1) * strip] = alpha
                    prow = slice(c * rs + i * strip, c * rs + (i + 1) * strip)
                    p_ref[r % 2, prow, :ns] = p.astype(BF16)
                    if ns < nk:
                        p_ref[r % 2, prow, ns:nk] = jnp.zeros((strip, nk - ns), BF16)

        def pv(r):
            nk = width(r)
            out = _dot(p_ref[r % 2, :, :nk], v_ref[0, pl.ds(r0, nk), :])
            rows = slice(r * rs, (r + 1) * rs)
            acc_ref[0, rows] = a_ref[r % 2, 0] * acc_ref[0, rows] + out[:rs]
            acc_ref[1, rows] = a_ref[r % 2, 1] * acc_ref[1, rows] + out[rs:]

        nr = tq // rs
        qk(0)
        for r in range(nr):
            if r + 1 < nr:
                qk(r + 1)
            softmax(r)
            pv(r)

    def body(kb, carry):
        step(kb, False)
        return carry

    lax.fori_loop(0, qi, body, 0)
    step(qi, True)

    lp = lam_ref[...]
    lam = (jnp.exp(jnp.sum(lp[0:1] * lp[1:2], axis=-1, keepdims=True))
           - jnp.exp(jnp.sum(lp[2:3] * lp[3:4], axis=-1, keepdims=True)) + lam_init)
    l0 = jnp.sum(l_ref[0], axis=-1, keepdims=True)
    l1 = jnp.sum(l_ref[1], axis=-1, keepdims=True)
    o = acc_ref[0] / l0 - lam * (acc_ref[1] / l1)
    o = _rms(o, sg_ref[...], SUBLN_EPS) * (1.0 - lam_init)
    o_ref[0] = o.astype(o_ref.dtype)


def _diff_attention(qkv3, lam_params, subln, lam_init, layer, *, heads, tq, rs):
    b, s, w3 = qkv3.shape
    hw = 2 * DIFF_HEAD_DIM
    return pl.pallas_call(
        functools.partial(_attn_kernel, tq=tq, rs=rs, lam_init=lam_init),
        out_shape=jax.ShapeDtypeStruct((b, s, w3 // 3), BF16),
        grid=(b, heads, s // tq),
        in_specs=[
            _stacked((4, DIFF_HEAD_DIM), layer, lambda bi, h, i: (0, 0)),
            pl.BlockSpec((1, tq, hw), lambda bi, h, i: (bi, i, h)),
            pl.BlockSpec((1, s, hw), lambda bi, h, i: (bi, 0, heads + h),
                         pipeline_mode=pl.Buffered(1)),
            pl.BlockSpec((1, s, hw), lambda bi, h, i: (bi, 0, 2 * heads + h),
                         pipeline_mode=pl.Buffered(1)),
            _stacked((1, hw), layer, lambda bi, h, i: (0, 0)),
        ],
        out_specs=pl.BlockSpec((1, tq, hw), lambda bi, h, i: (bi, i, h)),
        scratch_shapes=[pltpu.VMEM((2, tq, 1), F32), pltpu.VMEM((2, tq, LANES), F32),
                        pltpu.VMEM((2, tq, hw), F32), pltpu.VMEM((2, 2, rs, tq), F32),
                        pltpu.VMEM((2, 2 * rs, tq), BF16), pltpu.VMEM((2, 2, rs, 1), F32)],
        compiler_params=_cparams(("parallel", "parallel", "arbitrary")),
        name="diff_attention",
    )(lam_params, qkv3, qkv3, qkv3, subln)


def _out_proj_kernel(*refs):
    x_ref, w_ref, o_ref, wb_ref = refs[0], refs[-3], refs[-2], refs[-1]

    @pl.when(pl.program_id(0) == 0)
    def _():
        wb_ref[...] = w_ref[...].astype(BF16)

    acc = x_ref[...]
    k0 = 0
    for a_ref in refs[1:-3]:
        kw = a_ref.shape[1]
        acc = acc + _dot(a_ref[...], wb_ref[k0:k0 + kw, :])
        k0 += kw
    o_ref[...] = acc


def _out_proj(x, parts, w, layer, *, tm):
    t, d = x.shape
    k = w.shape[1]
    in_specs = [pl.BlockSpec((tm, d), lambda i: (i, 0))]
    in_specs += [pl.BlockSpec((tm, a.shape[1]), lambda i: (i, 0)) for a in parts]
    in_specs += [_stacked((k, d), layer, lambda i: (0, 0), pipeline_mode=pl.Buffered(1))]
    return pl.pallas_call(
        _out_proj_kernel,
        out_shape=jax.ShapeDtypeStruct((t, d), F32),
        grid=(t // tm,),
        in_specs=in_specs,
        out_specs=pl.BlockSpec((tm, d), lambda i: (i, 0)),
        scratch_shapes=[pltpu.VMEM((k, d), BF16)],
        compiler_params=_cparams(("arbitrary",)),
        name="out_proj",
    )(x, *parts, w)


def _ple_kernel(x_ref, p_ref, g_ref, wg_ref, wp_ref, fg_ref, o_ref, wgb_ref, wpb_ref, *,
                final_norm):
    @pl.when(pl.program_id(0) == 0)
    def _():
        wgb_ref[...] = wg_ref[...].astype(BF16)
        wpb_ref[...] = wp_ref[...].astype(BF16)

    x = x_ref[...]
    h = _rms(x, g_ref[...], NORM_EPS).astype(BF16)
    gate = jax.nn.sigmoid(_dot(h, wgb_ref[...]))
    y = x + _dot(p_ref[...].astype(BF16), wpb_ref[...]) * gate
    if final_norm:
        y = _rms(y, fg_ref[...], NORM_EPS)
    o_ref[...] = y


def _ple(x, p, gain, w_gate, w_proj, final_gain, layer, *, tm, final_norm):
    t, d = x.shape
    pd = p.shape[2]
    once = pl.Buffered(1)
    return pl.pallas_call(
        functools.partial(_ple_kernel, final_norm=final_norm),
        out_shape=jax.ShapeDtypeStruct((t, d), F32),
        grid=(t // tm,),
        in_specs=[
            pl.BlockSpec((tm, d), lambda i: (i, 0)),
            _stacked((tm, pd), layer, lambda i: (i, 0)),
            _stacked((1, d), layer, lambda i: (0, 0)),
            _stacked((d, d), layer, lambda i: (0, 0), pipeline_mode=once),
            _stacked((pd, d), layer, lambda i: (0, 0), pipeline_mode=once),
            pl.BlockSpec((1, d), lambda i: (0, 0)),
        ],
        out_specs=pl.BlockSpec((tm, d), lambda i: (i, 0)),
        scratch_shapes=[pltpu.VMEM((d, d), BF16), pltpu.VMEM((pd, d), BF16)],
        compiler_params=_cparams(("arbitrary",)),
        name="ple",
    )(x, p, gain, w_gate, w_proj, final_gain)


def _hgrn_kernel(q_ref, f_ref, v_ref, g_ref, lb_ref, ng_ref, o_ref, st_ref, qb_ref, gb_ref,
                 hb_ref, ob_ref, *, tt, hb):
    c16 = HGRN_CHUNK
    blk = HGRN_BLOCK
    per = blk // c16
    half = c16 // 2
    hd = HGRN_HEAD_DIM
    width = hb * hd
    log2e = math.log2(math.e)

    @pl.when(pl.program_id(2) == 0)
    def _():
        st_ref[...] = jnp.zeros(st_ref.shape, F32)

    lb = lb_ref[...]
    log_lb = jnp.log(lb)
    log_1m_lb = jnp.log1p(-lb)
    row = lax.broadcasted_iota(jnp.int32, (c16, width), 0)

    def rows_of(c, n=c16):
        return pl.ds(pl.multiple_of(c * n, n), n)

    def gates(i, lowest):
        off = jnp.zeros((1, width), F32)
        for j in range(per):
            rows = rows_of(i * per + j)
            q = q_ref[0, rows, :]
            qb_ref[rows, :] = _silu(q)
            ff = f_ref[0, rows, :]
            log_sig = jnp.minimum(ff, 0.0) - jnp.log(1.0 + jnp.exp(-jnp.abs(ff)))
            b = log_1m_lb + log_sig
            log_f = jnp.maximum(log_lb, b) + jnp.log(1.0 + jnp.exp(-jnp.abs(log_lb - b)))
            log_k = b - ff
            cum = log_f
            shift = 1
            while shift < c16:
                cum = cum + jnp.where(row >= shift, pltpu.roll(cum, shift, axis=0), 0.0)
                shift *= 2
            g2 = cum * log2e
            gb_ref[rows, :] = g2
            hb_ref[rows, :] = (log_k - cum) * log2e
            ob_ref[rows, :] = jnp.broadcast_to(off, (c16, width))
            off = off + g2[c16 - 1:c16, :]
        return jnp.minimum(lowest, off)

    lowest = lax.fori_loop(0, tt // blk, gates, jnp.zeros((1, width), F32))

    def finish(rows, h, out):
        cols = slice(h * hd, (h + 1) * hd)
        gate = g_ref[0, rows, cols]
        y = _rms(out, ng_ref[...], NORM_EPS) * _silu(gate)
        o_ref[0, rows, cols] = y.astype(o_ref.dtype)

    ones = jnp.ones((hd, hd), BF16)
    t8 = lax.broadcasted_iota(jnp.int32, (half, hd), 0)

    def intra_pairs(qc, g2, n2, vc):
        g_top, g_bot = g2[:half], g2[half:]
        q_top, q_bot = qc[:half], qc[half:]
        lhs = []
        for s in range(half):
            ns = n2[s:s + 1, :]
            d_top = g_top + ns
            if s > 0:
                d_top = jnp.where(t8 >= s, d_top, -jnp.inf)
            p = jnp.concatenate([q_top * jnp.exp2(d_top), q_bot * jnp.exp2(g_bot + ns)], axis=0)
            lhs.append(p.astype(BF16))
        bots = []
        for s in range(half, c16):
            d_bot = g_bot + n2[s:s + 1, :]
            if s > half:
                d_bot = jnp.where(t8 >= s - half, d_bot, -jnp.inf)
            bots.append(q_bot * jnp.exp2(d_bot))
        for j in range(0, half, 2):
            lhs.append(jnp.concatenate([bots[j], bots[j + 1]], axis=0).astype(BF16))
        red = _dot(jnp.concatenate(lhs, axis=0), ones)
        top_terms = [red[s * c16:s * c16 + half] * vc[s:s + 1, :] for s in range(half)]
        bot_terms = [red[s * c16 + half:(s + 1) * c16] * vc[s:s + 1, :] for s in range(half)]
        base = half * c16
        bot_terms += [red[base + j * half:base + (j + 1) * half] * vc[half + j:half + j + 1, :]
                      for j in range(half)]
        return jnp.concatenate([_tree_sum(top_terms), _tree_sum(bot_terms)], axis=0)

    def step_pairs(c, carry):
        rows = rows_of(c)
        for h in range(hb):
            cols = slice(h * hd, (h + 1) * hd)
            g2, n2, qc, vc = gb_ref[rows, cols], hb_ref[rows, cols], qb_ref[rows, cols], v_ref[0, rows, cols]
            state = st_ref[h]
            inter = _dot_nt((qc * jnp.exp2(g2)).astype(BF16), state.astype(BF16))
            finish(rows, h, inter + intra_pairs(qc, g2, n2, vc))
            g_last = g2[c16 - 1:c16, :]
            k_dec = jnp.exp2(g_last + n2).astype(BF16)
            st_ref[h] = state * jnp.exp2(g_last) + _dot_tn(vc.astype(BF16), k_dec)
        return carry

    tril = (lax.broadcasted_iota(jnp.int32, (blk, blk), 0)
            >= lax.broadcasted_iota(jnp.int32, (blk, blk), 1))

    def block_factored(i, carry):
        rows = rows_of(i, blk)
        q_dec, k_inv, k_dec, v_bf, decay = [], [], [], [], []
        for h in range(hb):
            cols = slice(h * hd, (h + 1) * hd)
            off = ob_ref[rows, cols]
            g = gb_ref[rows, cols] + off
            n = hb_ref[rows, cols] - off
            g_last = g[blk - 1:blk, :]
            q_dec.append((qb_ref[rows, cols] * jnp.exp2(g)).astype(BF16))
            k_inv.append(jnp.exp2(n).astype(BF16))
            k_dec.append(jnp.exp2(g_last + n).astype(BF16))
            v_bf.append(v_ref[0, rows, cols].astype(BF16))
            decay.append(jnp.exp2(g_last))
        scores = [_dot_nt(q_dec[h], k_inv[h]) for h in range(hb)]
        inter = [_dot_nt(q_dec[h], st_ref[h].astype(BF16)) for h in range(hb)]
        for h in range(hb):
            st_ref[h] = st_ref[h] * decay[h] + _dot_tn(v_bf[h], k_dec[h])
        for h in range(hb):
            causal = jnp.where(tril, scores[h], 0.0).astype(BF16)
            finish(rows, h, inter[h] + _dot(causal, v_bf[h]))
        return carry

    mild = jnp.min(lowest) >= -HGRN_SAFE_BITS

    @pl.when(mild)
    def _():
        lax.fori_loop(0, tt // blk, block_factored, 0)

    @pl.when(jnp.logical_not(mild))
    def _():
        lax.fori_loop(0, tt // c16, step_pairs, 0)


def _hgrn_scan(proj3, lb, norm_gain, layer, *, heads, tt, hb):
    b, s, w4 = proj3.shape
    hd = HGRN_HEAD_DIM
    width = hb * hd
    ng = heads // hb

    def sec(k):
        return pl.BlockSpec((1, tt, width), lambda bi, hg, i: (bi, i, k * ng + hg))

    return pl.pallas_call(
        functools.partial(_hgrn_kernel, tt=tt, hb=hb),
        out_shape=jax.ShapeDtypeStruct((b, s, w4 // 4), BF16),
        grid=(b, ng, s // tt),
        in_specs=[sec(0), sec(1), sec(2), sec(3),
                  pl.BlockSpec((1, width), lambda bi, hg, i: (0, hg)),
                  _stacked((1, hd), layer, lambda bi, hg, i: (0, 0))],
        out_specs=pl.BlockSpec((1, tt, width), lambda bi, hg, i: (bi, i, hg)),
        scratch_shapes=[pltpu.VMEM((hb, hd, hd), F32), pltpu.VMEM((tt, width), F32),
                        pltpu.VMEM((tt, width), F32), pltpu.VMEM((tt, width), F32),
                        pltpu.VMEM((tt, width), F32)],
        compiler_params=_cparams(("parallel", "parallel", "arbitrary")),
        name="hgrn_scan",
    )(proj3, proj3, proj3, proj3, lb, norm_gain)


def _tiles(t, seq, hgrn_heads):
    big = min(1024, t)
    small = min(512, t)
    return {
        "ffn": dict(tm=big, tf=256),
        "ab_proj": dict(tm=big, tn=512),
        "norm_matmul": dict(tm=min(2048, t), tn=512),
        "pool": dict(ts=min(512, seq)),
        "attention": dict(tq=min(2048, seq), rs=min(512, seq)),
        "out_proj": dict(tm=small),
        "ple": dict(tm=small),
        "hgrn": dict(tt=min(256, seq), hb=hgrn_heads),
    }


def kernel(x, p, positions, ffn1_norm, ffn1_w_gate, ffn1_w_up, ffn1_w_down, mix_norm, ffn2_norm, ffn2_w_gate, ffn2_w_up, ffn2_w_down, ple_norm, ple_w_gate, ple_w_proj, ab_w_in, pool_w, pool_scale, diff_lambda, diff_subln, ab_w_out, hgrn_w_in, hgrn_lower_bounds, hgrn_norm, hgrn_w_out, final_norm):
    bsz, seq, d = x.shape
    depth = p.shape[0]
    t = bsz * seq
    tile = _tiles(t, seq, d // HGRN_HEAD_DIM)

    def gains(a):
        return a.reshape(a.shape[0], 1, a.shape[-1]).astype(F32)

    half = ROT_DIM // 2
    inv_freq = ROPE_THETA ** (-jnp.arange(0, ROT_DIM, 2, dtype=F32) / ROT_DIM)
    zeros = jnp.zeros((LANES - ROT_DIM,), F32)
    freq = jnp.concatenate([inv_freq, inv_freq, zeros]).reshape(1, LANES)
    sign = jnp.concatenate([-jnp.ones((half,), F32), jnp.ones((half,), F32), zeros]).reshape(1, LANES)
    pos = positions.reshape(t, 1).astype(F32)

    lbs = jax.nn.softmax(hgrn_lower_bounds.astype(F32), axis=0)
    lbs = jnp.cumsum(lbs, axis=0) - lbs[0]

    ffn1_g, ffn2_g, mix_g, ple_g = gains(ffn1_norm), gains(ffn2_norm), gains(mix_norm), gains(ple_norm)
    p3 = p.reshape(depth, t, p.shape[-1])
    xt = x.reshape(t, d)
    for i in range(depth):
        xt = _ffn(xt, ffn1_g, ffn1_w_gate, ffn1_w_up, ffn1_w_down, i, **tile["ffn"])
        if i % 2 == 0:
            e = i // 2
            lam_init = 0.8 - 0.6 * math.exp(-0.3 * i)
            u, qkv = _ab_proj(xt, mix_g, ab_w_in, pos, freq, sign, i, e, **tile["ab_proj"])
            pw = u.shape[1]
            a_out = _pool(u.reshape(bsz, seq, pw), pool_w, gains(pool_scale), e, **tile["pool"])
            heads = qkv.shape[1] // (3 * 2 * DIFF_HEAD_DIM)
            b_out = _diff_attention(qkv.reshape(bsz, seq, -1), diff_lambda.astype(F32),
                                    gains(diff_subln), lam_init, e, heads=heads,
                                    **tile["attention"])
            xt = _out_proj(xt, [a_out.reshape(t, -1), b_out.reshape(t, -1)], ab_w_out, e, **tile["out_proj"])
        else:
            o = i // 2
            proj = _norm_matmul(xt, mix_g, hgrn_w_in, i, o, **tile["norm_matmul"])
            heads = d // HGRN_HEAD_DIM
            mixed = _hgrn_scan(proj.reshape(bsz, seq, -1), lbs[i].reshape(1, -1), gains(hgrn_norm),
                               o, heads=heads, **tile["hgrn"])
            xt = _out_proj(xt, [mixed.reshape(t, -1)], hgrn_w_out, o, **tile["out_proj"])
        xt = _ffn(xt, ffn2_g, ffn2_w_gate, ffn2_w_up, ffn2_w_down, i, **tile["ffn"])
        xt = _ple(xt, p3, ple_g, ple_w_gate, ple_w_proj, final_norm.reshape(1, -1).astype(F32), i,
                  final_norm=(i == depth - 1), **tile["ple"])
    return xt.reshape(bsz, seq, d)
```

```python
import functools
import math

import jax
import jax.numpy as jnp
from jax import lax
from jax.experimental import pallas as pl
from jax.experimental.pallas import tpu as pltpu

F32 = jnp.float32
BF16 = jnp.bfloat16

NORM_EPS = 1e-6
SUBLN_EPS = 1e-5
POOL_WINDOWS = (2, 4, 8, 16)
POOL_HALO = 16
DIFF_HEAD_DIM = 128
ROT_DIM = DIFF_HEAD_DIM // 4
ROPE_THETA = 500000.0
HGRN_HEAD_DIM = 128
HGRN_CHUNK = 16
HGRN_BLOCK = 64
HGRN_SAFE_BITS = 100.0

LANES = 128
VMEM_LIMIT = 56 * 1024 * 1024


def _cparams(sem):
    return pltpu.CompilerParams(dimension_semantics=sem, vmem_limit_bytes=VMEM_LIMIT)


def _stacked(block, layer, index, **kwargs):
    return pl.BlockSpec((None,) + tuple(block), lambda *g: (layer,) + tuple(index(*g)), **kwargs)


def _rms(x, gain, eps):
    ms = jnp.mean(x * x, axis=-1, keepdims=True)
    return x * lax.rsqrt(ms + eps) * gain


def _silu(x):
    h = 0.5 * x
    return h + h * jnp.tanh(h)


def _dot(a, b):
    return jnp.dot(a, b, preferred_element_type=F32)


def _dot_nt(a, b):
    return lax.dot_general(a, b, (((1,), (1,)), ((), ())), preferred_element_type=F32)


def _dot_tn(a, b):
    return lax.dot_general(a, b, (((0,), (0,)), ((), ())), preferred_element_type=F32)


def _tree_sum(terms):
    while len(terms) > 1:
        terms = [terms[i] + terms[i + 1] for i in range(0, len(terms) - 1, 2)] + (
            [terms[-1]] if len(terms) % 2 else [])
    return terms[0]


def _ffn_kernel(x_ref, g_ref, wg_ref, wu_ref, wd_ref, o_ref, h_ref):
    j = pl.program_id(1)

    def down_proj():
        h = h_ref[...]
        a = _dot(h, wg_ref[...].astype(BF16))
        b = _dot(h, wu_ref[...].astype(BF16))
        act = (_silu(a) * b).astype(BF16)
        return _dot(act, wd_ref[...].astype(BF16))

    last = pl.num_programs(1) - 1

    @pl.when(j == 0)
    def _():
        h_ref[...] = _rms(x_ref[...], g_ref[...], NORM_EPS).astype(BF16)
        o_ref[...] = down_proj()

    @pl.when(jnp.logical_and(j > 0, j < last))
    def _():
        o_ref[...] += down_proj()

    @pl.when(j == last)
    def _():
        o_ref[...] = x_ref[...] + 0.5 * (o_ref[...] + down_proj())


def _ffn(x, gain, w_gate, w_up, w_down, layer, *, tm, tf):
    t, d = x.shape
    f = w_gate.shape[2]
    assert f // tf >= 2, "the first and the last hidden tile must be different grid steps"
    return pl.pallas_call(
        _ffn_kernel,
        out_shape=jax.ShapeDtypeStruct((t, d), F32),
        grid=(t // tm, f // tf),
        in_specs=[
            pl.BlockSpec((tm, d), lambda i, j: (i, 0)),
            _stacked((1, d), layer, lambda i, j: (0, 0)),
            _stacked((d, tf), layer, lambda i, j: (0, j)),
            _stacked((d, tf), layer, lambda i, j: (0, j)),
            _stacked((tf, d), layer, lambda i, j: (j, 0)),
        ],
        out_specs=pl.BlockSpec((tm, d), lambda i, j: (i, 0)),
        scratch_shapes=[pltpu.VMEM((tm, d), BF16)],
        compiler_params=_cparams(("parallel", "arbitrary")),
        name="ffn",
    )(x, gain, w_gate, w_up, w_down)


def _norm_matmul_kernel(x_ref, g_ref, w_ref, o_ref, h_ref):
    j = pl.program_id(1)

    def proj():
        return _dot(h_ref[...], w_ref[...].astype(BF16)).astype(o_ref.dtype)

    @pl.when(j == 0)
    def _():
        h_ref[...] = _rms(x_ref[...], g_ref[...], NORM_EPS).astype(BF16)
        o_ref[...] = proj()

    @pl.when(j > 0)
    def _():
        o_ref[...] = proj()


def _norm_matmul(x, gain, w, layer, w_layer, *, tm, tn):
    t, d = x.shape
    n = w.shape[2]
    return pl.pallas_call(
        _norm_matmul_kernel,
        out_shape=jax.ShapeDtypeStruct((t, n), F32),
        grid=(t // tm, n // tn),
        in_specs=[
            pl.BlockSpec((tm, d), lambda i, j: (i, 0), pipeline_mode=pl.Buffered(1)),
            _stacked((1, d), layer, lambda i, j: (0, 0)),
            _stacked((d, tn), w_layer, lambda i, j: (0, j)),
        ],
        out_specs=pl.BlockSpec((tm, tn), lambda i, j: (i, j)),
        scratch_shapes=[pltpu.VMEM((tm, d), BF16)],
        compiler_params=_cparams(("parallel", "arbitrary")),
        name="norm_matmul",
    )(x, gain, w)


def _rope(y, cos, sin_signed):
    lane = lax.broadcasted_iota(jnp.int32, cos.shape, 1)
    half = ROT_DIM // 2
    outs = []
    for c in range(y.shape[1] // LANES):
        t = y[:, c * LANES:(c + 1) * LANES]
        swapped = jnp.where(lane < half, pltpu.roll(t, LANES - half, axis=1),
                            pltpu.roll(t, half, axis=1))
        outs.append(t * cos + swapped * sin_signed)
    return jnp.concatenate(outs, axis=1)


def _ab_proj_kernel(x_ref, g_ref, w_ref, pos_ref, freq_ref, sign_ref, u_ref, qkv_ref,
                    h_ref, cos_ref, sin_ref, *, q_scale, per):
    j = pl.program_id(1)
    sec = j // per

    def proj():
        return _dot(h_ref[...], w_ref[...].astype(BF16))

    @pl.when(j == 0)
    def _():
        h_ref[...] = _rms(x_ref[...], g_ref[...], NORM_EPS).astype(BF16)
        ang = pos_ref[...] * freq_ref[...]
        cos_ref[...] = jnp.cos(ang)
        sin_ref[...] = jnp.sin(ang) * sign_ref[...]
        u_ref[...] = proj()

    @pl.when(jnp.logical_and(sec == 0, j > 0))
    def _():
        u_ref[...] = proj()

    @pl.when(sec == 1)
    def _():
        qkv_ref[...] = (_rope(proj(), cos_ref[...], sin_ref[...]) * q_scale).astype(BF16)

    @pl.when(sec == 2)
    def _():
        qkv_ref[...] = _rope(proj(), cos_ref[...], sin_ref[...]).astype(BF16)

    @pl.when(sec == 3)
    def _():
        qkv_ref[...] = proj().astype(BF16)


def _ab_proj(x, gain, w, pos, freq, sign, layer, w_layer, *, tm, tn):
    t, d = x.shape
    n = w.shape[2]
    per = n // 4 // tn
    q_scale = DIFF_HEAD_DIM ** -0.5 * math.log2(math.e)
    return pl.pallas_call(
        functools.partial(_ab_proj_kernel, q_scale=q_scale, per=per),
        out_shape=(jax.ShapeDtypeStruct((t, n // 4), F32),
                   jax.ShapeDtypeStruct((t, 3 * n // 4), BF16)),
        grid=(t // tm, 4 * per),
        in_specs=[
            pl.BlockSpec((tm, d), lambda i, j: (i, 0), pipeline_mode=pl.Buffered(1)),
            _stacked((1, d), layer, lambda i, j: (0, 0)),
            _stacked((d, tn), w_layer, lambda i, j: (0, j)),
            pl.BlockSpec((tm, 1), lambda i, j: (i, 0)),
            pl.BlockSpec((1, LANES), lambda i, j: (0, 0)),
            pl.BlockSpec((1, LANES), lambda i, j: (0, 0)),
        ],
        out_specs=(
            pl.BlockSpec((tm, tn), lambda i, j: (i, jnp.minimum(j, per - 1))),
            pl.BlockSpec((tm, tn), lambda i, j: (i, jnp.maximum(j - per, 0))),
        ),
        scratch_shapes=[pltpu.VMEM((tm, d), BF16), pltpu.VMEM((tm, LANES), F32),
                        pltpu.VMEM((tm, LANES), F32)],
        compiler_params=_cparams(("parallel", "arbitrary")),
        name="ab_proj",
    )(x, gain, w, pos, freq, sign)


def _pool_kernel(u_ref, w_ref, s_ref, o_ref, ext_ref, *, ts):
    i = pl.program_id(1)

    @pl.when(i == 0)
    def _():
        ext_ref[0:POOL_HALO, :] = jnp.zeros((POOL_HALO, ext_ref.shape[1]), F32)

    ext_ref[POOL_HALO:, :] = u_ref[0]
    gd = w_ref.shape[1]
    row = lax.broadcasted_iota(jnp.int32, (ts, gd), 0) + i * ts
    for g, win in enumerate(POOL_WINDOWS):
        cols = slice(g * gd, (g + 1) * gd)
        acc = ext_ref[:, cols]
        shift = 1
        while shift < win:
            acc = acc + pltpu.roll(acc, shift, axis=0)
            shift *= 2
        cnt = jnp.minimum(row + 1, win).astype(F32)
        v = u_ref[0, :, cols]
        dlt = acc[POOL_HALO:, :] / cnt - v
        y = _dot(dlt.astype(BF16), w_ref[g].astype(BF16)) * s_ref[:, cols]
        o_ref[0, :, cols] = y.astype(o_ref.dtype)
    ext_ref[0:POOL_HALO, :] = ext_ref[ts:ts + POOL_HALO, :]


def _pool(u3, pool_w, pool_scale, layer, *, ts):
    b, s, pw = u3.shape
    _, g, gd, _ = pool_w.shape
    return pl.pallas_call(
        functools.partial(_pool_kernel, ts=ts),
        out_shape=jax.ShapeDtypeStruct((b, s, pw), BF16),
        grid=(b, s // ts),
        in_specs=[
            pl.BlockSpec((1, ts, pw), lambda bi, i: (bi, i, 0)),
            _stacked((g, gd, gd), layer, lambda bi, i: (0, 0, 0)),
            _stacked((1, pw), layer, lambda bi, i: (0, 0)),
        ],
        out_specs=pl.BlockSpec((1, ts, pw), lambda bi, i: (bi, i, 0)),
        scratch_shapes=[pltpu.VMEM((ts + POOL_HALO, pw), F32)],
        compiler_params=_cparams(("parallel", "arbitrary")),
        name="pool",
    )(u3, pool_w, pool_scale)


def _attn_kernel(lam_ref, q_ref, k_ref, v_ref, sg_ref, o_ref, m_ref, l_ref, acc_ref, s_ref,
                 p_ref, a_ref, *, tq, rs, lam_init):
    dh = DIFF_HEAD_DIM
    strip = 16
    qi = pl.program_id(2)
    m_ref[...] = jnp.full(m_ref.shape, -jnp.inf, F32)
    l_ref[...] = jnp.zeros(l_ref.shape, F32)
    acc_ref[...] = jnp.zeros(acc_ref.shape, F32)

    def step(kb, masked):
        r0 = pl.multiple_of(kb * tq, tq)

        def width(r):
            return (r + 1) * rs if masked else tq

        def qk(r):
            nk = width(r)
            for c in range(2):
                comp = slice(c * dh, (c + 1) * dh)
                s_ref[r % 2, c, :, :nk] = _dot_nt(q_ref[0, r * rs:(r + 1) * rs, comp],
                                                  k_ref[0, pl.ds(r0, nk), comp])

        def softmax(r):
            nk = width(r)
            for c in range(2):
                for i in range(rs // strip):
                    lo = r * rs + i * strip
                    rows = slice(lo, lo + strip)
                    ns = min(nk, -(-(lo + strip) // LANES) * LANES) if masked else nk
                    s = s_ref[r % 2, c, i * strip:(i + 1) * strip, :ns]
                    if masked:
                        row = lax.broadcasted_iota(jnp.int32, s.shape, 0) + lo
                        col = lax.broadcasted_iota(jnp.int32, s.shape, 1)
                        s = jnp.where(col <= row, s, -jnp.inf)
                    m_prev = m_ref[c, rows]
                    m_new = jnp.maximum(m_prev, jnp.max(s, axis=-1, keepdims=True))
                    alpha = jnp.exp2(m_prev - m_new)
                    p = jnp.exp2(s - m_new)
                    part = _tree_sum([p[:, j * LANES:(j + 1) * LANES] for j in range(ns // LANES)])
                    l_ref[c, rows] = alpha * l_ref[c, rows] + part
                    m_ref[c, rows] = m_new
                    a_ref[r % 2, c, i * strip:(i + 1) * strip] = alpha
                    prow = slice(c * rs + i * strip, c * rs + (i + 1) * strip)
                    p_ref[r % 2, prow, :ns] = p.astype(BF16)
                    if ns < nk:
                        p_ref[r % 2, prow, ns:nk] = jnp.zeros((strip, nk - ns), BF16)

        def pv(r):
            nk = width(r)
            out = _dot(p_ref[r % 2, :, :nk], v_ref[0, pl.ds(r0, nk), :])
            rows = slice(r * rs, (r + 1) * rs)
            acc_ref[0, rows] = a_ref[r % 2, 0] * acc_ref[0, rows] + out[:rs]
            acc_ref[1, rows] = a_ref[r % 2, 1] * acc_ref[1, rows] + out[rs:]

        nr = tq // rs
        qk(0)
        for r in range(nr):
            if r + 1 < nr:
                qk(r + 1)
            softmax(r)
            pv(r)

    def body(kb, carry):
        step(kb, False)
        return carry

    lax.fori_loop(0, qi, body, 0)
    step(qi, True)

    lp = lam_ref[...]
    lam = (jnp.exp(jnp.sum(lp[0:1] * lp[1:2], axis=-1, keepdims=True))
           - jnp.exp(jnp.sum(lp[2:3] * lp[3:4], axis=-1, keepdims=True)) + lam_init)
    l0 = jnp.sum(l_ref[0], axis=-1, keepdims=True)
    l1 = jnp.sum(l_ref[1], axis=-1, keepdims=True)
    o = acc_ref[0] / l0 - lam * (acc_ref[1] / l1)
    o = _rms(o, sg_ref[...], SUBLN_EPS) * (1.0 - lam_init)
    o_ref[0] = o.astype(o_ref.dtype)


def _diff_attention(qkv3, lam_params, subln, lam_init, layer, *, heads, tq, rs):
    b, s, w3 = qkv3.shape
    hw = 2 * DIFF_HEAD_DIM
    return pl.pallas_call(
        functools.partial(_attn_kernel, tq=tq, rs=rs, lam_init=lam_init),
        out_shape=jax.ShapeDtypeStruct((b, s, w3 // 3), BF16),
        grid=(b, heads, s // tq),
        in_specs=[
            _stacked((4, DIFF_HEAD_DIM), layer, lambda bi, h, i: (0, 0)),
            pl.BlockSpec((1, tq, hw), lambda bi, h, i: (bi, i, h)),
            pl.BlockSpec((1, s, hw), lambda bi, h, i: (bi, 0, heads + h),
                         pipeline_mode=pl.Buffered(1)),
            pl.BlockSpec((1, s, hw), lambda bi, h, i: (bi, 0, 2 * heads + h),
                         pipeline_mode=pl.Buffered(1)),
            _stacked((1, hw), layer, lambda bi, h, i: (0, 0)),
        ],
        out_specs=pl.BlockSpec((1, tq, hw), lambda bi, h, i: (bi, i, h)),
        scratch_shapes=[pltpu.VMEM((2, tq, 1), F32), pltpu.VMEM((2, tq, LANES), F32),
                        pltpu.VMEM((2, tq, hw), F32), pltpu.VMEM((2, 2, rs, tq), F32),
                        pltpu.VMEM((2, 2 * rs, tq), BF16), pltpu.VMEM((2, 2, rs, 1), F32)],
        compiler_params=_cparams(("parallel", "parallel", "arbitrary")),
        name="diff_attention",
    )(lam_params, qkv3, qkv3, qkv3, subln)


def _out_proj_kernel(*refs):
    x_ref, w_ref, o_ref, wb_ref = refs[0], refs[-3], refs[-2], refs[-1]

    @pl.when(pl.program_id(0) == 0)
    def _():
        wb_ref[...] = w_ref[...].astype(BF16)

    acc = x_ref[...]
    k0 = 0
    for a_ref in refs[1:-3]:
        kw = a_ref.shape[1]
        acc = acc + _dot(a_ref[...], wb_ref[k0:k0 + kw, :])
        k0 += kw
    o_ref[...] = acc


def _out_proj(x, parts, w, layer, *, tm):
    t, d = x.shape
    k = w.shape[1]
    in_specs = [pl.BlockSpec((tm, d), lambda i: (i, 0))]
    in_specs += [pl.BlockSpec((tm, a.shape[1]), lambda i: (i, 0)) for a in parts]
    in_specs += [_stacked((k, d), layer, lambda i: (0, 0), pipeline_mode=pl.Buffered(1))]
    return pl.pallas_call(
        _out_proj_kernel,
        out_shape=jax.ShapeDtypeStruct((t, d), F32),
        grid=(t // tm,),
        in_specs=in_specs,
        out_specs=pl.BlockSpec((tm, d), lambda i: (i, 0)),
        scratch_shapes=[pltpu.VMEM((k, d), BF16)],
        compiler_params=_cparams(("arbitrary",)),
        name="out_proj",
    )(x, *parts, w)


def _ple_kernel(x_ref, p_ref, g_ref, wg_ref, wp_ref, fg_ref, o_ref, wgb_ref, wpb_ref, *,
                final_norm):
    @pl.when(pl.program_id(0) == 0)
    def _():
        wgb_ref[...] = wg_ref[...].astype(BF16)
        wpb_ref[...] = wp_ref[...].astype(BF16)

    x = x_ref[...]
    h = _rms(x, g_ref[...], NORM_EPS).astype(BF16)
    gate = jax.nn.sigmoid(_dot(h, wgb_ref[...]))
    y = x + _dot(p_ref[...].astype(BF16), wpb_ref[...]) * gate
    if final_norm:
        y = _rms(y, fg_ref[...], NORM_EPS)
    o_ref[...] = y


def _ple(x, p, gain, w_gate, w_proj, final_gain, layer, *, tm, final_norm):
    t, d = x.shape
    pd = p.shape[2]
    once = pl.Buffered(1)
    return pl.pallas_call(
        functools.partial(_ple_kernel, final_norm=final_norm),
        out_shape=jax.ShapeDtypeStruct((t, d), F32),
        grid=(t // tm,),
        in_specs=[
            pl.BlockSpec((tm, d), lambda i: (i, 0)),
            _stacked((tm, pd), layer, lambda i: (i, 0)),
            _stacked((1, d), layer, lambda i: (0, 0)),
            _stacked((d, d), layer, lambda i: (0, 0), pipeline_mode=once),
            _stacked((pd, d), layer, lambda i: (0, 0), pipeline_mode=once),
            pl.BlockSpec((1, d), lambda i: (0, 0)),
        ],
        out_specs=pl.BlockSpec((tm, d), lambda i: (i, 0)),
        scratch_shapes=[pltpu.VMEM((d, d), BF16), pltpu.VMEM((pd, d), BF16)],
        compiler_params=_cparams(("arbitrary",)),
        name="ple",
    )(x, p, gain, w_gate, w_proj, final_gain)


def _hgrn_kernel(q_ref, f_ref, v_ref, g_ref, lb_ref, ng_ref, o_ref, st_ref, qb_ref, gb_ref,
                 hb_ref, ob_ref, *, tt, hb):
    c16 = HGRN_CHUNK
    blk = HGRN_BLOCK
    per = blk // c16
    half = c16 // 2
    hd = HGRN_HEAD_DIM
    width = hb * hd
    log2e = math.log2(math.e)

    @pl.when(pl.program_id(2) == 0)
    def _():
        st_ref[...] = jnp.zeros(st_ref.shape, F32)

    lb = lb_ref[...]
    log_lb = jnp.log(lb)
    log_1m_lb = jnp.log1p(-lb)
    row = lax.broadcasted_iota(jnp.int32, (c16, width), 0)

    def rows_of(c, n=c16):
        return pl.ds(pl.multiple_of(c * n, n), n)

    def gates(i, lowest):
        off = jnp.zeros((1, width), F32)
        for j in range(per):
            rows = rows_of(i * per + j)
            q = q_ref[0, rows, :]
            qb_ref[rows, :] = _silu(q)
            ff = f_ref[0, rows, :]
            log_sig = jnp.minimum(ff, 0.0) - jnp.log(1.0 + jnp.exp(-jnp.abs(ff)))
            b = log_1m_lb + log_sig
            log_f = jnp.maximum(log_lb, b) + jnp.log(1.0 + jnp.exp(-jnp.abs(log_lb - b)))
            log_k = b - ff
            cum = log_f
            shift = 1
            while shift < c16:
                cum = cum + jnp.where(row >= shift, pltpu.roll(cum, shift, axis=0), 0.0)
                shift *= 2
            g2 = cum * log2e
            gb_ref[rows, :] = g2
            hb_ref[rows, :] = (log_k - cum) * log2e
            ob_ref[rows, :] = jnp.broadcast_to(off, (c16, width))
            off = off + g2[c16 - 1:c16, :]
        return jnp.minimum(lowest, off)

    lowest = lax.fori_loop(0, tt // blk, gates, jnp.zeros((1, width), F32))

    def finish(rows, h, out):
        cols = slice(h * hd, (h + 1) * hd)
        gate = g_ref[0, rows, cols]
        y = _rms(out, ng_ref[...], NORM_EPS) * _silu(gate)
        o_ref[0, rows, cols] = y.astype(o_ref.dtype)

    ones = jnp.ones((hd, hd), BF16)
    t8 = lax.broadcasted_iota(jnp.int32, (half, hd), 0)

    def intra_pairs(qc, g2, n2, vc):
        g_top, g_bot = g2[:half], g2[half:]
        q_top, q_bot = qc[:half], qc[half:]
        lhs = []
        for s in range(half):
            ns = n2[s:s + 1, :]
            d_top = g_top + ns
            if s > 0:
                d_top = jnp.where(t8 >= s, d_top, -jnp.inf)
            p = jnp.concatenate([q_top * jnp.exp2(d_top), q_bot * jnp.exp2(g_bot + ns)], axis=0)
            lhs.append(p.astype(BF16))
        bots = []
        for s in range(half, c16):
            d_bot = g_bot + n2[s:s + 1, :]
            if s > half:
                d_bot = jnp.where(t8 >= s - half, d_bot, -jnp.inf)
            bots.append(q_bot * jnp.exp2(d_bot))
        for j in range(0, half, 2):
            lhs.append(jnp.concatenate([bots[j], bots[j + 1]], axis=0).astype(BF16))
        red = _dot(jnp.concatenate(lhs, axis=0), ones)
        top_terms = [red[s * c16:s * c16 + half] * vc[s:s + 1, :] for s in range(half)]
        bot_terms = [red[s * c16 + half:(s + 1) * c16] * vc[s:s + 1, :] for s in range(half)]
        base = half * c16
        bot_terms += [red[base + j * half:base + (j + 1) * half] * vc[half + j:half + j + 1, :]
                      for j in range(half)]
        return jnp.concatenate([_tree_sum(top_terms), _tree_sum(bot_terms)], axis=0)

    def step_pairs(c, carry):
        rows = rows_of(c)
        for h in range(hb):
            cols = slice(h * hd, (h + 1) * hd)
            g2, n2, qc, vc = gb_ref[rows, cols], hb_ref[rows, cols], qb_ref[rows, cols], v_ref[0, rows, cols]
            state = st_ref[h]
            inter = _dot_nt((qc * jnp.exp2(g2)).astype(BF16), state.astype(BF16))
            finish(rows, h, inter + intra_pairs(qc, g2, n2, vc))
            g_last = g2[c16 - 1:c16, :]
            k_dec = jnp.exp2(g_last + n2).astype(BF16)
            st_ref[h] = state * jnp.exp2(g_last) + _dot_tn(vc.astype(BF16), k_dec)
        return carry

    tril = (lax.broadcasted_iota(jnp.int32, (blk, blk), 0)
            >= lax.broadcasted_iota(jnp.int32, (blk, blk), 1))

    def block_factored(i, carry):
        rows = rows_of(i, blk)
        q_dec, k_inv, k_dec, v_bf, decay = [], [], [], [], []
        for h in range(hb):
            cols = slice(h * hd, (h + 1) * hd)
            off = ob_ref[rows, cols]
            g = gb_ref[rows, cols] + off
            n = hb_ref[rows, cols] - off
            g_last = g[blk - 1:blk, :]
            q_dec.append((qb_ref[rows, cols] * jnp.exp2(g)).astype(BF16))
            k_inv.append(jnp.exp2(n).astype(BF16))
            k_dec.append(jnp.exp2(g_last + n).astype(BF16))
            v_bf.append(v_ref[0, rows, cols].astype(BF16))
            decay.append(jnp.exp2(g_last))
        scores = [_dot_nt(q_dec[h], k_inv[h]) for h in range(hb)]
        inter = [_dot_nt(q_dec[h], st_ref[h].astype(BF16)) for h in range(hb)]
        for h in range(hb):
            st_ref[h] = st_ref[h] * decay[h] + _dot_tn(v_bf[h], k_dec[h])
        for h in range(hb):
            causal = jnp.where(tril, scores[h], 0.0).astype(BF16)
            finish(rows, h, inter[h] + _dot(causal, v_bf[h]))
        return carry

    mild = jnp.min(lowest) >= -HGRN_SAFE_BITS

    @pl.when(mild)
    def _():
        lax.fori_loop(0, tt // blk, block_factored, 0)

    @pl.when(jnp.logical_not(mild))
    def _():
        lax.fori_loop(0, tt // c16, step_pairs, 0)


def _hgrn_scan(proj3, lb, norm_gain, layer, *, heads, tt, hb):
    b, s, w4 = proj3.shape
    hd = HGRN_HEAD_DIM
    width = hb * hd
    ng = heads // hb

    def sec(k):
        return pl.BlockSpec((1, tt, width), lambda bi, hg, i: (bi, i, k * ng + hg))

    return pl.pallas_call(
        functools.partial(_hgrn_kernel, tt=tt, hb=hb),
        out_shape=jax.ShapeDtypeStruct((b, s, w4 // 4), BF16),
        grid=(b, ng, s // tt),
        in_specs=[sec(0), sec(1), sec(2), sec(3),
                  pl.BlockSpec((1, width), lambda bi, hg, i: (0, hg)),
                  _stacked((1, hd), layer, lambda bi, hg, i: (0, 0))],
        out_specs=pl.BlockSpec((1, tt, width), lambda bi, hg, i: (bi, i, hg)),
        scratch_shapes=[pltpu.VMEM((hb, hd, hd), F32), pltpu.VMEM((tt, width), F32),
                        pltpu.VMEM((tt, width), F32), pltpu.VMEM((tt, width), F32),
                        pltpu.VMEM((tt, width), F32)],
        compiler_params=_cparams(("parallel", "parallel", "arbitrary")),
        name="hgrn_scan",
    )(proj3, proj3, proj3, proj3, lb, norm_gain)


def _tiles(t, seq, hgrn_heads):
    big = min(1024, t)
    small = min(512, t)
    return {
        "ffn": dict(tm=big, tf=256),
        "ab_proj": dict(tm=min(2048, t), tn=256),
        "norm_matmul": dict(tm=min(2048, t), tn=512),
        "pool": dict(ts=min(512, seq)),
        "attention": dict(tq=min(2048, seq), rs=min(512, seq)),
        "out_proj": dict(tm=small),
        "ple": dict(tm=small),
        "hgrn": dict(tt=min(256, seq), hb=hgrn_heads),
    }


def kernel(x, p, positions, ffn1_norm, ffn1_w_gate, ffn1_w_up, ffn1_w_down, mix_norm, ffn2_norm, ffn2_w_gate, ffn2_w_up, ffn2_w_down, ple_norm, ple_w_gate, ple_w_proj, ab_w_in, pool_w, pool_scale, diff_lambda, diff_subln, ab_w_out, hgrn_w_in, hgrn_lower_bounds, hgrn_norm, hgrn_w_out, final_norm):
    bsz, seq, d = x.shape
    depth = p.shape[0]
    t = bsz * seq
    tile = _tiles(t, seq, d // HGRN_HEAD_DIM)

    def gains(a):
        return a.reshape(a.shape[0], 1, a.shape[-1]).astype(F32)

    half = ROT_DIM // 2
    inv_freq = ROPE_THETA ** (-jnp.arange(0, ROT_DIM, 2, dtype=F32) / ROT_DIM)
    zeros = jnp.zeros((LANES - ROT_DIM,), F32)
    freq = jnp.concatenate([inv_freq, inv_freq, zeros]).reshape(1, LANES)
    sign = jnp.concatenate([-jnp.ones((half,), F32), jnp.ones((half,), F32), zeros]).reshape(1, LANES)
    pos = positions.reshape(t, 1).astype(F32)

    lbs = jax.nn.softmax(hgrn_lower_bounds.astype(F32), axis=0)
    lbs = jnp.cumsum(lbs, axis=0) - lbs[0]

    ffn1_g, ffn2_g, mix_g, ple_g = gains(ffn1_norm), gains(ffn2_norm), gains(mix_norm), gains(ple_norm)
    p3 = p.reshape(depth, t, p.shape[-1])
    xt = x.reshape(t, d)
    for i in range(depth):
        xt = _ffn(xt, ffn1_g, ffn1_w_gate, ffn1_w_up, ffn1_w_down, i, **tile["ffn"])
        if i % 2 == 0:
            e = i // 2
            lam_init = 0.8 - 0.6 * math.exp(-0.3 * i)
            u, qkv = _ab_proj(xt, mix_g, ab_w_in, pos, freq, sign, i, e, **tile["ab_proj"])
            pw = u.shape[1]
            a_out = _pool(u.reshape(bsz, seq, pw), pool_w, gains(pool_scale), e, **tile["pool"])
            heads = qkv.shape[1] // (3 * 2 * DIFF_HEAD_DIM)
            b_out = _diff_attention(qkv.reshape(bsz, seq, -1), diff_lambda.astype(F32),
                                    gains(diff_subln), lam_init, e, heads=heads,
                                    **tile["attention"])
            xt = _out_proj(xt, [a_out.reshape(t, -1), b_out.reshape(t, -1)], ab_w_out, e, **tile["out_proj"])
        else:
            o = i // 2
            proj = _norm_matmul(xt, mix_g, hgrn_w_in, i, o, **tile["norm_matmul"])
            heads = d // HGRN_HEAD_DIM
            mixed = _hgrn_scan(proj.reshape(bsz, seq, -1), lbs[i].reshape(1, -1), gains(hgrn_norm),
                               o, heads=heads, **tile["hgrn"])
            xt = _out_proj(xt, [mixed.reshape(t, -1)], hgrn_w_out, o, **tile["out_proj"])
        xt = _ffn(xt, ffn2_g, ffn2_w_gate, ffn2_w_up, ffn2_w_down, i, **tile["ffn"])
        xt = _ple(xt, p3, ple_g, ple_w_gate, ple_w_proj, final_norm.reshape(1, -1).astype(F32), i,
                  final_norm=(i == depth - 1), **tile["ple"])
    return xt.reshape(bsz, seq, d)
```
